```python
import jax, jax.numpy as jnp
from jax import lax
import numpy as np

D_MODEL = 1024
BATCH = 4
SEQ = 4096
DEPTH = 4

HEAD_DIM = 64
N_HEADS = D_MODEL // HEAD_DIM
N_HEADS_MOBA = N_HEADS // 2
N_HEADS_DIL = N_HEADS - N_HEADS_MOBA
D_MOBA = N_HEADS_MOBA * HEAD_DIM
D_DIL = N_HEADS_DIL * HEAD_DIM
D_FF = 4 * D_MODEL
MOBA_BLOCK = 256
MOBA_TOPK = 3
MOBA_Q_BLOCK = 32
DILATED_CONFIGS = ((128, 1), (512, 4), (2048, 16))
DIL_Q_BLOCK = 64
ROPE_THETA = 10000.0
NORM_EPS = 1e-6
NEG_INF = -1e30

kernel_name = "hymba_moba_dilated_hybrid"


def rms_norm(x, g):
    xf = x.astype(jnp.float32)
    y = xf * lax.rsqrt(jnp.mean(xf * xf, axis=-1, keepdims=True) + NORM_EPS)
    return (y * g.astype(jnp.float32)).astype(x.dtype)


def rope_tables(seq):
    inv = 1.0 / (ROPE_THETA ** (jnp.arange(0, HEAD_DIM, 2, dtype=jnp.float32) / HEAD_DIM))
    ang = jnp.arange(seq, dtype=jnp.float32)[:, None] * inv[None, :]
    return jnp.cos(ang), jnp.sin(ang)


def apply_rope(x, cos, sin):
    x1, x2 = jnp.split(x.astype(jnp.float32), 2, axis=-1)
    out = jnp.concatenate([x1 * cos - x2 * sin, x2 * cos + x1 * sin], axis=-1)
    return out.astype(x.dtype)


def split_heads(t, n_heads):
    b, s, _ = t.shape
    return t.reshape(b, s, n_heads, HEAD_DIM).transpose(0, 2, 1, 3)


def merge_heads(t):
    b, h, s, hd = t.shape
    return t.transpose(0, 2, 1, 3).reshape(b, s, h * hd)


def moba_attention(q, k, v):
    b, h, s, hd = q.shape
    nb = -(-s // MOBA_BLOCK)
    pad = nb * MOBA_BLOCK - s
    k_blocks = jnp.pad(k, ((0, 0), (0, 0), (0, pad), (0, 0))).reshape(b, h, nb, MOBA_BLOCK, hd)
    v_blocks = jnp.pad(v, ((0, 0), (0, 0), (0, pad), (0, 0))).reshape(b, h, nb, MOBA_BLOCK, hd)
    k_mean = jnp.mean(k_blocks.astype(jnp.float32), axis=3)
    q_block_id = jnp.arange(s) // MOBA_BLOCK
    gate = jnp.einsum('bhsd,bhnd->bhsn', q.astype(jnp.float32), k_mean)
    past = jnp.arange(nb)[None, :] < q_block_id[:, None]
    gate = jnp.where(past, gate, NEG_INF)
    topk = min(MOBA_TOPK, nb)
    _, sel = lax.top_k(gate, topk)
    sel_valid = sel < q_block_id[:, None]
    gather_blocks = jax.vmap(jax.vmap(lambda blocks, idx: blocks[idx]))
    scale = HEAD_DIM ** -0.5

    def one_block(start):
        qb = lax.dynamic_slice_in_dim(q, start, MOBA_Q_BLOCK, axis=2)
        selb = lax.dynamic_slice_in_dim(sel, start, MOBA_Q_BLOCK, axis=2)
        validb = lax.dynamic_slice_in_dim(sel_valid, start, MOBA_Q_BLOCK, axis=2)
        t = start + jnp.arange(MOBA_Q_BLOCK)
        own = start // MOBA_BLOCK
        k_own = lax.dynamic_index_in_dim(k_blocks, own, axis=2, keepdims=False)
        v_own = lax.dynamic_index_in_dim(v_blocks, own, axis=2, keepdims=False)
        k_sel = gather_blocks(k_blocks, selb)
        v_sel = gather_blocks(v_blocks, selb)
        s_sel = jnp.einsum('bhqd,bhqkjd->bhqkj', qb, k_sel).astype(jnp.float32) * scale
        s_sel = jnp.where(validb[..., None], s_sel, NEG_INF).reshape(b, h, MOBA_Q_BLOCK, topk * MOBA_BLOCK)
        s_own = jnp.einsum('bhqd,bhjd->bhqj', qb, k_own).astype(jnp.float32) * scale
        own_pos = own * MOBA_BLOCK + jnp.arange(MOBA_BLOCK)
        s_own = jnp.where(own_pos[None, :] <= t[:, None], s_own, NEG_INF)
        p = jax.nn.softmax(jnp.concatenate([s_sel, s_own], axis=-1), axis=-1).astype(v.dtype)
        p_sel = p[..., :topk * MOBA_BLOCK].reshape(b, h, MOBA_Q_BLOCK, topk, MOBA_BLOCK)
        p_own = p[..., topk * MOBA_BLOCK:]
        return (jnp.einsum('bhqkj,bhqkjd->bhqd', p_sel, v_sel)
                + jnp.einsum('bhqj,bhjd->bhqd', p_own, v_own))

    starts = jnp.arange(s // MOBA_Q_BLOCK) * MOBA_Q_BLOCK
    out = lax.map(one_block, starts)
    return jnp.moveaxis(out, 0, 2).reshape(b, h, s, hd)


def dilated_attention(q, k, v):
    b, h, s, hd = q.shape
    scale = HEAD_DIM ** -0.5

    def one_block(start):
        qb = lax.dynamic_slice_in_dim(q, start, DIL_Q_BLOCK, axis=2)
        t = start + jnp.arange(DIL_Q_BLOCK)
        outs, lses = [], []
        for window, dilation in DILATED_CONFIGS:
            n_keys = window // dilation + 1
            idx = t[:, None] - dilation * jnp.arange(n_keys)[None, :]
            valid = idx >= 0
            idx = jnp.maximum(idx, 0)
            kg = jnp.take(k, idx, axis=2)
            vg = jnp.take(v, idx, axis=2)
            sc = jnp.einsum('bhqd,bhqnd->bhqn', qb, kg).astype(jnp.float32) * scale
            sc = jnp.where(valid, sc, NEG_INF)
            lse = jax.nn.logsumexp(sc, axis=-1)
            p = jnp.exp(sc - lse[..., None])
            outs.append(jnp.einsum('bhqn,bhqnd->bhqd', p, vg.astype(jnp.float32)))
            lses.append(lse)
        w = jax.nn.softmax(jnp.stack(lses, axis=0), axis=0)
        out = jnp.einsum('gbhq,gbhqd->bhqd', w, jnp.stack(outs, axis=0))
        return out.astype(v.dtype)

    starts = jnp.arange(s // DIL_Q_BLOCK) * DIL_Q_BLOCK
    out = lax.map(one_block, starts)
    return jnp.moveaxis(out, 0, 2).reshape(b, h, s, hd)


def hybrid_layer(x, attn_norm, w_in, moba_out_norm, dil_out_norm, w_out,
                 mlp_norm, w_up, w_down, cos, sin):
    h = rms_norm(x, attn_norm)
    proj = h @ w_in
    qa, ka, va, qd, kd, vd = jnp.split(
        proj, [D_MOBA, 2 * D_MOBA, 3 * D_MOBA, 3 * D_MOBA + D_DIL, 3 * D_MOBA + 2 * D_DIL], axis=-1)
    qa = apply_rope(split_heads(qa, N_HEADS_MOBA), cos, sin)
    ka = apply_rope(split_heads(ka, N_HEADS_MOBA), cos, sin)
    va = split_heads(va, N_HEADS_MOBA)
    qd = apply_rope(split_heads(qd, N_HEADS_DIL), cos, sin)
    kd = apply_rope(split_heads(kd, N_HEADS_DIL), cos, sin)
    vd = split_heads(vd, N_HEADS_DIL)
    out_a = rms_norm(merge_heads(moba_attention(qa, ka, va)), moba_out_norm)
    out_d = rms_norm(merge_heads(dilated_attention(qd, kd, vd)), dil_out_norm)
    x = x + jnp.concatenate([out_a, out_d], axis=-1) @ w_out
    h = rms_norm(x, mlp_norm)
    x = x + jnp.square(jax.nn.relu(h @ w_up)) @ w_down
    return x


def setup_inputs(seed: int = 0) -> dict:
    key = jax.random.key(seed)
    ks = jax.random.split(key, 10)
    f32 = jnp.float32
    return {
        "x": jax.random.normal(ks[0], (BATCH, SEQ, D_MODEL), f32),
        "attn_norm": 1.0 + 0.02 * jax.random.normal(ks[1], (DEPTH, D_MODEL), f32),
        "w_in": jax.random.normal(ks[2], (DEPTH, D_MODEL, 3 * D_MODEL), f32) * D_MODEL ** -0.5,
        "moba_out_norm": 1.0 + 0.02 * jax.random.normal(ks[3], (DEPTH, D_MOBA), f32),
        "dil_out_norm": 1.0 + 0.02 * jax.random.normal(ks[4], (DEPTH, D_DIL), f32),
        "w_out": jax.random.normal(ks[5], (DEPTH, D_MODEL, D_MODEL), f32) * (2 * D_MODEL) ** -0.5,
        "mlp_norm": 1.0 + 0.02 * jax.random.normal(ks[6], (DEPTH, D_MODEL), f32),
        "w_up": jax.random.normal(ks[7], (DEPTH, D_MODEL, D_FF), f32) * D_MODEL ** -0.5,
        "w_down": jax.random.normal(ks[8], (DEPTH, D_FF, D_MODEL), f32) * D_FF ** -0.5,
        "final_norm": 1.0 + 0.02 * jax.random.normal(ks[9], (D_MODEL,), f32),
    }


def reference(x, attn_norm, w_in, moba_out_norm, dil_out_norm, w_out,
              mlp_norm, w_up, w_down, final_norm):
    cos, sin = rope_tables(x.shape[1])
    for l in range(DEPTH):
        x = hybrid_layer(x, attn_norm[l], w_in[l], moba_out_norm[l], dil_out_norm[l], w_out[l],
                         mlp_norm[l], w_up[l], w_down[l], cos, sin)
    return rms_norm(x, final_norm)
```

```python
import functools
import math

import jax
import jax.numpy as jnp
from jax import lax
from jax.experimental import pallas as pl
from jax.experimental.pallas import tpu as pltpu

HEAD_DIM = 64
HEADS_PER_PAIR = 2
PAIR_DIM = HEADS_PER_PAIR * HEAD_DIM
MOBA_BLOCK = 256
MOBA_TOPK = 3
DILATED_CONFIGS = ((128, 1), (512, 4), (2048, 16))
ROPE_THETA = 10000.0
NORM_EPS = 1e-6
NEG_INF = -1e30
ATT_TILE = 256
DIL_TILES_BACK = max(w for w, _ in DILATED_CONFIGS) // ATT_TILE
VMEM_LIMIT_BYTES = 56 * 1024 * 1024

_f32 = jnp.float32
_bf16 = jnp.bfloat16


def _rms_norm_rows(x, g):
    return x * lax.rsqrt(jnp.mean(x * x, axis=-1, keepdims=True) + NORM_EPS) * g


def _inproj_kernel(x_ref, g_ref, w_ref, cos_ref, sin_ref,
                   qta_ref, ka_ref, vta_ref, qtd_ref, kd_ref, vtd_ref, *, n_pairs, scale):
    tm = x_ref.shape[0]
    nt = tm // ATT_TILE
    h = _rms_norm_rows(x_ref[...], g_ref[...]).astype(_bf16)
    proj = jnp.dot(h, w_ref[...], preferred_element_type=_f32)
    cos = cos_ref[...]
    sin = sin_ref[...]
    lane = lax.broadcasted_iota(jnp.int32, (tm, PAIR_DIM), 1)
    first_half = (lane % HEAD_DIM) < (HEAD_DIM // 2)

    def rope(c):
        partner = jnp.where(first_half,
                            pltpu.roll(c, PAIR_DIM - HEAD_DIM // 2, 1),
                            pltpu.roll(c, HEAD_DIM // 2, 1))
        return c * cos + partner * sin

    def chunk(group, p):
        c0 = (group * n_pairs + p) * PAIR_DIM
        return proj[:, c0:c0 + PAIR_DIM]

    def put_transposed(ref, p, c):
        for t in range(nt):
            ref[p, t] = c[t * ATT_TILE:(t + 1) * ATT_TILE, :].T.astype(_bf16)

    def put_rows(ref, p, c):
        for t in range(nt):
            ref[p, t] = c[t * ATT_TILE:(t + 1) * ATT_TILE, :].astype(_bf16)

    for p in range(n_pairs):
        put_transposed(qta_ref, p, rope(chunk(0, p)) * scale)
        put_rows(ka_ref, p, rope(chunk(1, p)))
        put_transposed(vta_ref, p, chunk(2, p))
        put_transposed(qtd_ref, p, rope(chunk(3, p)) * scale)
        put_rows(kd_ref, p, rope(chunk(4, p)))
        put_transposed(vtd_ref, p, chunk(5, p))


def _inproj(x, g, w, cos_t, sin_t, *, tm):
    b, s, d = x.shape
    n_pairs = w.shape[1] // (6 * PAIR_DIM)
    nq = s // ATT_TILE
    nt = tm // ATT_TILE
    t_shape = jax.ShapeDtypeStruct((b, n_pairs, nq, PAIR_DIM, ATT_TILE), _bf16)
    r_shape = jax.ShapeDtypeStruct((b, n_pairs, nq, ATT_TILE, PAIR_DIM), _bf16)
    t_spec = pl.BlockSpec((None, n_pairs, nt, PAIR_DIM, ATT_TILE), lambda bi, si: (bi, 0, si, 0, 0))
    r_spec = pl.BlockSpec((None, n_pairs, nt, ATT_TILE, PAIR_DIM), lambda bi, si: (bi, 0, si, 0, 0))
    return pl.pallas_call(
        functools.partial(_inproj_kernel, n_pairs=n_pairs, scale=HEAD_DIM ** -0.5),
        grid=(b, s // tm),
        in_specs=[
            pl.BlockSpec((None, tm, d), lambda bi, si: (bi, si, 0)),
            pl.BlockSpec((1, d), lambda bi, si: (0, 0)),
            pl.BlockSpec(w.shape, lambda bi, si: (0, 0), pipeline_mode=pl.Buffered(1)),
            pl.BlockSpec((tm, PAIR_DIM), lambda bi, si: (si, 0)),
            pl.BlockSpec((tm, PAIR_DIM), lambda bi, si: (si, 0)),
        ],
        out_specs=[t_spec, r_spec, t_spec, t_spec, r_spec, t_spec],
        out_shape=[t_shape, r_shape, t_shape, t_shape, r_shape, t_shape],
        compiler_params=pltpu.CompilerParams(
            dimension_semantics=("parallel", "parallel"), vmem_limit_bytes=VMEM_LIMIT_BYTES),
        name="inproj",
    )(x, g, w, cos_t, sin_t)


def _head_rows(qt, head):
    row = lax.broadcasted_iota(jnp.int32, qt.shape, 0)
    mine = (row >= head * HEAD_DIM) & (row < (head + 1) * HEAD_DIM)
    return jnp.where(mine, qt, jnp.zeros_like(qt))


def _attend_first(s, vth, m_ref, l_ref, acc_ref, head):
    m = jnp.max(s, axis=0, keepdims=True)
    p = jnp.exp(s - m)
    m_ref[head] = m
    l_ref[head] = jnp.sum(p, axis=0, keepdims=True)
    acc_ref[head] = jnp.dot(vth, p.astype(_bf16), preferred_element_type=_f32)


def _attend_next(s, vth, m_ref, l_ref, acc_ref, head):
    m_old = m_ref[head]
    m = jnp.maximum(m_old, jnp.max(s, axis=0, keepdims=True))
    alpha = jnp.exp(m_old - m)
    p = jnp.exp(s - m)
    m_ref[head] = m
    l_ref[head] = alpha * l_ref[head] + jnp.sum(p, axis=0, keepdims=True)
    acc_ref[head] = alpha * acc_ref[head] + jnp.dot(vth, p.astype(_bf16), preferred_element_type=_f32)


def _write_out(o_ref, qi, l_ref, acc_ref):
    outs = [acc_ref[hd] / l_ref[hd] for hd in range(HEADS_PER_PAIR)]
    o_t = jnp.concatenate(outs, axis=0)
    row0 = pl.multiple_of(qi * ATT_TILE, ATT_TILE)
    o_ref[pl.ds(row0, ATT_TILE), :] = o_t.T.astype(o_ref.dtype)


def _moba_kernel(qt_ref, k_ref, vt_ref, o_ref, kmean_ref, sel_ref, m_ref, l_ref, acc_ref):
    nq = qt_ref.shape[0]

    for j in range(nq):
        kmean_ref[j:j + 1, :] = jnp.mean(k_ref[j].astype(_f32), axis=0, keepdims=True)
    km = kmean_ref[...]
    km_hi = km.astype(_bf16)
    km_lo = (km - km_hi.astype(_f32)).astype(_bf16)

    key_i = lax.broadcasted_iota(jnp.int32, (ATT_TILE, ATT_TILE), 0)
    qry_i = lax.broadcasted_iota(jnp.int32, (ATT_TILE, ATT_TILE), 1)
    causal = key_i <= qry_i
    blk = lax.broadcasted_iota(jnp.int32, (nq, ATT_TILE), 0)

    def q_tile(qi, carry):
        qt = qt_ref[qi]
        qts = [_head_rows(qt, hd) for hd in range(HEADS_PER_PAIR)]
        past = blk < qi
        for hd in range(HEADS_PER_PAIR):
            gate = (jnp.dot(km_hi, qts[hd], preferred_element_type=_f32)
                    + jnp.dot(km_lo, qts[hd], preferred_element_type=_f32))
            gate = jnp.where(past, gate, NEG_INF)
            rank = jnp.zeros(gate.shape, _f32)
            for i in range(nq):
                gi = gate[i:i + 1, :]
                rank = rank + jnp.where(blk > i, jnp.where(gi >= gate, 1.0, 0.0),
                                        jnp.where(gi > gate, 1.0, 0.0))
            sel_ref[hd] = jnp.where(past, jnp.where(rank < MOBA_TOPK, 0.0, NEG_INF), NEG_INF)

        k_own = k_ref[qi]
        for hd in range(HEADS_PER_PAIR):
            s = jnp.dot(k_own, qts[hd], preferred_element_type=_f32)
            s = jnp.where(causal, s, NEG_INF)
            _attend_first(s, vt_ref[qi, hd * HEAD_DIM:(hd + 1) * HEAD_DIM, :], m_ref, l_ref, acc_ref, hd)

        def past_block(j, c):
            kj = k_ref[j]
            for hd in range(HEADS_PER_PAIR):
                s = jnp.dot(kj, qts[hd], preferred_element_type=_f32)
                s = s + sel_ref[hd, pl.ds(j, 1), :]
                _attend_next(s, vt_ref[j, hd * HEAD_DIM:(hd + 1) * HEAD_DIM, :], m_ref, l_ref, acc_ref, hd)
            return c

        lax.fori_loop(0, qi, past_block, 0)
        _write_out(o_ref, qi, l_ref, acc_ref)
        return carry

    lax.fori_loop(0, nq, q_tile, 0)


def _attention_call(kernel, name, qt, k, vt, extra_inputs, extra_specs, scratch):
    b, n_pairs, nq = qt.shape[:3]
    s = nq * ATT_TILE
    t_spec = pl.BlockSpec((None, None, nq, PAIR_DIM, ATT_TILE), lambda bi, pi: (bi, pi, 0, 0, 0))
    r_spec = pl.BlockSpec((None, None, nq, ATT_TILE, PAIR_DIM), lambda bi, pi: (bi, pi, 0, 0, 0))
    return pl.pallas_call(
        kernel,
        grid=(b, n_pairs),
        in_specs=[t_spec, r_spec, t_spec] + extra_specs,
        out_specs=pl.BlockSpec((None, s, PAIR_DIM), lambda bi, pi: (bi, 0, pi)),
        out_shape=jax.ShapeDtypeStruct((b, s, n_pairs * PAIR_DIM), _bf16),
        scratch_shapes=scratch + [
            pltpu.VMEM((HEADS_PER_PAIR, 1, ATT_TILE), _f32),
            pltpu.VMEM((HEADS_PER_PAIR, 1, ATT_TILE), _f32),
            pltpu.VMEM((HEADS_PER_PAIR, HEAD_DIM, ATT_TILE), _f32),
        ],
        compiler_params=pltpu.CompilerParams(
            dimension_semantics=("parallel", "parallel"), vmem_limit_bytes=VMEM_LIMIT_BYTES),
        name=name,
    )(qt, k, vt, *extra_inputs)


def _moba_attention(qt, k, vt):
    nq = qt.shape[2]
    scratch = [pltpu.VMEM((nq, PAIR_DIM), _f32),
               pltpu.VMEM((HEADS_PER_PAIR, nq, ATT_TILE), _f32)]
    return _attention_call(_moba_kernel, "moba", qt, k, vt, [], [], scratch)


def _dilated_bias_table():
    d = jnp.arange(DIL_TILES_BACK + 1, dtype=jnp.int32)[:, None, None]
    key_i = jnp.arange(ATT_TILE, dtype=jnp.int32)[None, :, None]
    qry_i = jnp.arange(ATT_TILE, dtype=jnp.int32)[None, None, :]
    delta = d * ATT_TILE + qry_i - key_i
    count = jnp.zeros(delta.shape, _f32)
    for window, dilation in DILATED_CONFIGS:
        hit = (delta >= 0) & (delta <= window) & (delta % dilation == 0)
        count = count + hit.astype(_f32)
    return jnp.where(count > 0, jnp.log(jnp.maximum(count, 1.0)), NEG_INF)


def _dilated_kernel(qt_ref, k_ref, vt_ref, bias_ref, o_ref, m_ref, l_ref, acc_ref):
    nq = qt_ref.shape[0]

    def q_tile(qi, carry):
        qt = qt_ref[qi]
        qts = [_head_rows(qt, hd) for hd in range(HEADS_PER_PAIR)]
        k_own = k_ref[qi]
        for hd in range(HEADS_PER_PAIR):
            s = jnp.dot(k_own, qts[hd], preferred_element_type=_f32) + bias_ref[0]
            _attend_first(s, vt_ref[qi, hd * HEAD_DIM:(hd + 1) * HEAD_DIM, :], m_ref, l_ref, acc_ref, hd)

        def past_tile(j, c):
            kj = k_ref[j]
            bias = bias_ref[qi - j]
            for hd in range(HEADS_PER_PAIR):
                s = jnp.dot(kj, qts[hd], preferred_element_type=_f32) + bias
                _attend_next(s, vt_ref[j, hd * HEAD_DIM:(hd + 1) * HEAD_DIM, :], m_ref, l_ref, acc_ref, hd)
            return c

        lax.fori_loop(jnp.maximum(qi - DIL_TILES_BACK, 0), qi, past_tile, 0)
        _write_out(o_ref, qi, l_ref, acc_ref)
        return carry

    lax.fori_loop(0, nq, q_tile, 0)


def _dilated_attention(qt, k, vt, bias):
    bias_spec = pl.BlockSpec(bias.shape, lambda bi, pi: (0, 0, 0), pipeline_mode=pl.Buffered(1))
    return _attention_call(_dilated_kernel, "dilated", qt, k, vt, [bias], [bias_spec], [])


def _post_kernel(x_ref, oa_ref, od_ref, ga_ref, gd_ref, wout_ref, gm_ref, wup_ref, wdown_ref,
                 gf_ref, o_ref, *, final_norm):
    da = oa_ref.shape[1]
    na = _rms_norm_rows(oa_ref[...].astype(_f32), ga_ref[...]).astype(_bf16)
    nd = _rms_norm_rows(od_ref[...].astype(_f32), gd_ref[...]).astype(_bf16)
    y = (x_ref[...]
         + jnp.dot(na, wout_ref[:da, :], preferred_element_type=_f32)
         + jnp.dot(nd, wout_ref[da:, :], preferred_element_type=_f32))
    h = _rms_norm_rows(y, gm_ref[...]).astype(_bf16)
    u = jnp.dot(h, wup_ref[...], preferred_element_type=_f32)
    u = jnp.square(jnp.maximum(u, 0.0)).astype(_bf16)
    y = y + jnp.dot(u, wdown_ref[...], preferred_element_type=_f32)
    if final_norm:
        y = _rms_norm_rows(y, gf_ref[...])
    o_ref[...] = y


def _post(x, oa, od, ga, gd, wout, gm, wup, wdown, gf, *, tm, final_norm):
    b, s, d = x.shape
    da = oa.shape[-1]
    dd = od.shape[-1]
    row = lambda bi, si: (bi, si, 0)
    const = lambda bi, si: (0, 0)
    whole = lambda a: pl.BlockSpec(a.shape, const, pipeline_mode=pl.Buffered(1))
    return pl.pallas_call(
        functools.partial(_post_kernel, final_norm=final_norm),
        grid=(b, s // tm),
        in_specs=[
            pl.BlockSpec((None, tm, d), row),
            pl.BlockSpec((None, tm, da), row),
            pl.BlockSpec((None, tm, dd), row),
            pl.BlockSpec((1, da), const),
            pl.BlockSpec((1, dd), const),
            whole(wout),
            pl.BlockSpec((1, d), const),
            whole(wup),
            whole(wdown),
            pl.BlockSpec((1, d), const),
        ],
        out_specs=pl.BlockSpec((None, tm, d), row),
        out_shape=jax.ShapeDtypeStruct((b, s, d), _f32),
        compiler_params=pltpu.CompilerParams(
            dimension_semantics=("parallel", "parallel"), vmem_limit_bytes=VMEM_LIMIT_BYTES),
        name="post",
    )(x, oa, od, ga, gd, wout, gm, wup, wdown, gf)


def _rope_tables(seq):
    inv = 1.0 / (ROPE_THETA ** (jnp.arange(0, HEAD_DIM, 2, dtype=_f32) / HEAD_DIM))
    ang = jnp.arange(seq, dtype=_f32)[:, None] * inv[None, :]
    cos, sin = jnp.cos(ang), jnp.sin(ang)
    reps = PAIR_DIM // (HEAD_DIM // 2)
    cos_t = jnp.tile(cos, (1, reps))
    sin_t = jnp.tile(jnp.concatenate([-sin, sin], axis=-1), (1, HEADS_PER_PAIR))
    return cos_t, sin_t


def kernel(x, attn_norm, w_in, moba_out_norm, dil_out_norm, w_out, mlp_norm, w_up, w_down, final_norm):
    depth = w_in.shape[0]
    seq = x.shape[1]
    cos_t, sin_t = _rope_tables(seq)
    bias = _dilated_bias_table()
    row = lambda v: v.reshape(1, -1)
    for l in range(depth):
        qta, ka, vta, qtd, kd, vtd = _inproj(
            x, row(attn_norm[l]), w_in[l].astype(_bf16), cos_t, sin_t, tm=512)
        out_a = _moba_attention(qta, ka, vta)
        out_d = _dilated_attention(qtd, kd, vtd, bias)
        x = _post(x, out_a, out_d, row(moba_out_norm[l]), row(dil_out_norm[l]),
                  w_out[l].astype(_bf16), row(mlp_norm[l]), w_up[l].astype(_bf16),
                  w_down[l].astype(_bf16), row(final_norm), tm=512, final_norm=(l == depth - 1))
    return x
```

```python
import functools
import math

import jax
import jax.numpy as jnp
from jax import lax
from jax.experimental import pallas as pl
from jax.experimental.pallas import tpu as pltpu

HEAD_DIM = 64
HEADS_PER_PAIR = 2
PAIR_DIM = HEADS_PER_PAIR * HEAD_DIM
MOBA_BLOCK = 256
MOBA_TOPK = 3
DILATED_CONFIGS = ((128, 1), (512, 4), (2048, 16))
ROPE_THETA = 10000.0
NORM_EPS = 1e-6
NEG_INF = -1e30
ATT_TILE = 256
ONES_ROWS = 16
DIL_TILES_BACK = max(w for w, _ in DILATED_CONFIGS) // ATT_TILE
VMEM_LIMIT_BYTES = 56 * 1024 * 1024

_f32 = jnp.float32
_bf16 = jnp.bfloat16


def _rms_norm_rows(x, g):
    return x * lax.rsqrt(jnp.mean(x * x, axis=-1, keepdims=True) + NORM_EPS) * g


def _inproj_kernel(x_ref, g_ref, w_ref, cos_ref, sin_ref,
                   qta_ref, ka_ref, vta_ref, qtd_ref, kd_ref, vtd_ref, *, n_pairs, scale):
    tm = x_ref.shape[0]
    nt = tm // ATT_TILE
    h = _rms_norm_rows(x_ref[...], g_ref[...]).astype(_bf16)
    proj = jnp.dot(h, w_ref[...], preferred_element_type=_f32)
    cos = cos_ref[...]
    sin = sin_ref[...]
    lane = lax.broadcasted_iota(jnp.int32, (tm, PAIR_DIM), 1)
    first_half = (lane % HEAD_DIM) < (HEAD_DIM // 2)

    def rope(c):
        partner = jnp.where(first_half,
                            pltpu.roll(c, PAIR_DIM - HEAD_DIM // 2, 1),
                            pltpu.roll(c, HEAD_DIM // 2, 1))
        return c * cos + partner * sin

    def chunk(group, p):
        c0 = (group * n_pairs + p) * PAIR_DIM
        return proj[:, c0:c0 + PAIR_DIM]

    def put_transposed(ref, p, c):
        for t in range(nt):
            ref[p, t] = c[t * ATT_TILE:(t + 1) * ATT_TILE, :].T.astype(_bf16)

    def put_rows(ref, p, c):
        for t in range(nt):
            ref[p, t] = c[t * ATT_TILE:(t + 1) * ATT_TILE, :].astype(_bf16)

    for p in range(n_pairs):
        put_transposed(qta_ref, p, rope(chunk(0, p)) * scale)
        put_rows(ka_ref, p, rope(chunk(1, p)))
        put_transposed(vta_ref, p, chunk(2, p))
        put_transposed(qtd_ref, p, rope(chunk(3, p)) * scale)
        put_rows(kd_ref, p, rope(chunk(4, p)))
        put_transposed(vtd_ref, p, chunk(5, p))


def _inproj(x, g, w, cos_t, sin_t, *, tm):
    b, s, d = x.shape
    n_pairs = w.shape[1] // (6 * PAIR_DIM)
    nq = s // ATT_TILE
    nt = tm // ATT_TILE
    t_shape = jax.ShapeDtypeStruct((b, n_pairs, nq, PAIR_DIM, ATT_TILE), _bf16)
    r_shape = jax.ShapeDtypeStruct((b, n_pairs, nq, ATT_TILE, PAIR_DIM), _bf16)
    t_spec = pl.BlockSpec((None, n_pairs, nt, PAIR_DIM, ATT_TILE), lambda bi, si: (bi, 0, si, 0, 0))
    r_spec = pl.BlockSpec((None, n_pairs, nt, ATT_TILE, PAIR_DIM), lambda bi, si: (bi, 0, si, 0, 0))
    return pl.pallas_call(
        functools.partial(_inproj_kernel, n_pairs=n_pairs, scale=HEAD_DIM ** -0.5 * math.log2(math.e)),
        grid=(b, s // tm),
        in_specs=[
            pl.BlockSpec((None, tm, d), lambda bi, si: (bi, si, 0)),
            pl.BlockSpec((1, d), lambda bi, si: (0, 0)),
            pl.BlockSpec(w.shape, lambda bi, si: (0, 0), pipeline_mode=pl.Buffered(1)),
            pl.BlockSpec((tm, PAIR_DIM), lambda bi, si: (si, 0)),
            pl.BlockSpec((tm, PAIR_DIM), lambda bi, si: (si, 0)),
        ],
        out_specs=[t_spec, r_spec, t_spec, t_spec, r_spec, t_spec],
        out_shape=[t_shape, r_shape, t_shape, t_shape, r_shape, t_shape],
        compiler_params=pltpu.CompilerParams(
            dimension_semantics=("parallel", "parallel"), vmem_limit_bytes=VMEM_LIMIT_BYTES),
        name="inproj",
    )(x, g, w, cos_t, sin_t)


def _head_rows(qt, head):
    row = lax.broadcasted_iota(jnp.int32, qt.shape, 0)
    mine = (row >= head * HEAD_DIM) & (row < (head + 1) * HEAD_DIM)
    return jnp.where(mine, qt, jnp.zeros_like(qt))


def _write_out(o_ref, qi, outs):
    o_t = jnp.concatenate(outs, axis=0)
    row0 = pl.multiple_of(qi * ATT_TILE, ATT_TILE)
    o_ref[pl.ds(row0, ATT_TILE), :] = o_t.T.astype(o_ref.dtype)


def _attention_call(kernel, name, qt, k, vt, extra_inputs, extra_specs, scratch):
    b, n_pairs, nq = qt.shape[:3]
    s = nq * ATT_TILE
    t_spec = pl.BlockSpec((None, None, nq, PAIR_DIM, ATT_TILE), lambda bi, pi: (bi, pi, 0, 0, 0))
    r_spec = pl.BlockSpec((None, None, nq, ATT_TILE, PAIR_DIM), lambda bi, pi: (bi, pi, 0, 0, 0))
    return pl.pallas_call(
        kernel,
        grid=(b, n_pairs),
        in_specs=[t_spec, r_spec, t_spec] + extra_specs,
        out_specs=pl.BlockSpec((None, s, PAIR_DIM), lambda bi, pi: (bi, 0, pi)),
        out_shape=jax.ShapeDtypeStruct((b, s, n_pairs * PAIR_DIM), _bf16),
        scratch_shapes=scratch,
        compiler_params=pltpu.CompilerParams(
            dimension_semantics=("parallel", "parallel"), vmem_limit_bytes=VMEM_LIMIT_BYTES),
        name=name,
    )(qt, k, vt, *extra_inputs)


def _moba_kernel(qt_ref, k_ref, vt_ref, aux_ref, o_ref,
                 kmean_ref, w_ref, sd_ref, s_ref, m_ref, acc_ref):
    nq = qt_ref.shape[0]
    chunk = 2 * ATT_TILE

    for j in range(nq):
        kmean_ref[j:j + 1, :] = jnp.mean(k_ref[j].astype(_f32), axis=0, keepdims=True)
    km = kmean_ref[...]
    km_hi = km.astype(_bf16)
    km_lo = (km - km_hi.astype(_f32)).astype(_bf16)
    for hd in range(HEADS_PER_PAIR):
        w_ref[hd, PAIR_DIM + nq:, :] = jnp.zeros((PAIR_DIM - nq, ATT_TILE), _bf16)

    key_i = lax.broadcasted_iota(jnp.int32, (ATT_TILE, ATT_TILE), 0)
    qry_i = lax.broadcasted_iota(jnp.int32, (ATT_TILE, ATT_TILE), 1)
    causal = key_i <= qry_i
    blk = lax.broadcasted_iota(jnp.int32, (nq, ATT_TILE), 0)
    ones = jnp.ones((ONES_ROWS, ATT_TILE), _bf16)

    def head_vt(j, hd):
        return jnp.concatenate([vt_ref[j, hd * HEAD_DIM:(hd + 1) * HEAD_DIM, :], ones], axis=0)

    def q_tile(qi, carry):
        qt = qt_ref[qi]
        past = blk < qi
        for hd in range(HEADS_PER_PAIR):
            qth = _head_rows(qt, hd)
            gate = (jnp.dot(km_hi, qth, preferred_element_type=_f32)
                    + jnp.dot(km_lo, qth, preferred_element_type=_f32))
            gate = jnp.where(past, gate, NEG_INF)
            rank = jnp.zeros(gate.shape, _f32)
            for i in range(nq):
                gi = gate[i:i + 1, :]
                rank = rank + jnp.where(blk > i, jnp.where(gi >= gate, 1.0, 0.0),
                                        jnp.where(gi > gate, 1.0, 0.0))
            sel = jnp.where(past, jnp.where(rank < MOBA_TOPK, 0.0, NEG_INF), NEG_INF)
            w_ref[hd, :PAIR_DIM, :] = qth
            w_ref[hd, PAIR_DIM:PAIR_DIM + nq, :] = sel.astype(_bf16)

        k_own = k_ref[qi]
        for hd in range(HEADS_PER_PAIR):
            s = jnp.dot(k_own, w_ref[hd, :PAIR_DIM, :], preferred_element_type=_f32)
            s = jnp.where(causal, s, NEG_INF)
            sd_ref[hd] = s
            m_ref[hd] = jnp.max(s, axis=0, keepdims=True)

        n_chunks = (qi + 1) // 2

        def score_chunk(c, carry):
            keys = k_ref[pl.ds(2 * c, 2)].reshape(chunk, PAIR_DIM)
            lhs = jnp.concatenate([keys, aux_ref[c]], axis=1)
            for hd in range(HEADS_PER_PAIR):
                s = jnp.dot(lhs, w_ref[hd], preferred_element_type=_f32)
                s_ref[hd, c] = s
                m_ref[hd] = jnp.maximum(m_ref[hd], jnp.max(s, axis=0, keepdims=True))
            return carry

        lax.fori_loop(0, n_chunks, score_chunk, 0)

        for hd in range(HEADS_PER_PAIR):
            p = jnp.exp2(sd_ref[hd] - m_ref[hd])
            acc_ref[hd] = jnp.dot(head_vt(qi, hd), p.astype(_bf16), preferred_element_type=_f32)

        def value_chunk(c, carry):
            for hd in range(HEADS_PER_PAIR):
                p = jnp.exp2(s_ref[hd, c] - m_ref[hd])
                vtc = jnp.concatenate([head_vt(2 * c, hd), head_vt(2 * c + 1, hd)], axis=1)
                acc_ref[hd] += jnp.dot(vtc, p.astype(_bf16), preferred_element_type=_f32)
            return carry

        lax.fori_loop(0, n_chunks, value_chunk, 0)
        _write_out(o_ref, qi, [acc_ref[hd, :HEAD_DIM, :] / acc_ref[hd, HEAD_DIM:HEAD_DIM + 1, :]
                               for hd in range(HEADS_PER_PAIR)])
        return carry

    lax.fori_loop(0, nq, q_tile, 0)


def _moba_block_onehot(nq):
    c = jnp.arange(nq // 2, dtype=jnp.int32)[:, None, None]
    r = jnp.arange(2 * ATT_TILE, dtype=jnp.int32)[None, :, None]
    lane = jnp.arange(PAIR_DIM, dtype=jnp.int32)[None, None, :]
    return (lane == 2 * c + r // ATT_TILE).astype(_bf16)


def _moba_attention(qt, k, vt):
    nq = qt.shape[2]
    aux = _moba_block_onehot(nq)
    aux_spec = pl.BlockSpec(aux.shape, lambda bi, pi: (0, 0, 0), pipeline_mode=pl.Buffered(1))
    scratch = [pltpu.VMEM((nq, PAIR_DIM), _f32),
               pltpu.VMEM((HEADS_PER_PAIR, 2 * PAIR_DIM, ATT_TILE), _bf16),
               pltpu.VMEM((HEADS_PER_PAIR, ATT_TILE, ATT_TILE), _f32),
               pltpu.VMEM((HEADS_PER_PAIR, nq // 2, 2 * ATT_TILE, ATT_TILE), _f32),
               pltpu.VMEM((HEADS_PER_PAIR, 1, ATT_TILE), _f32),
               pltpu.VMEM((HEADS_PER_PAIR, HEAD_DIM + ONES_ROWS, ATT_TILE), _f32)]
    return _attention_call(_moba_kernel, "moba", qt, k, vt, [aux], [aux_spec], scratch)


def _dilated_bias_table():
    d = jnp.arange(DIL_TILES_BACK + 2, dtype=jnp.int32)[:, None, None]
    key_i = jnp.arange(ATT_TILE, dtype=jnp.int32)[None, :, None]
    qry_i = jnp.arange(ATT_TILE, dtype=jnp.int32)[None, None, :]
    delta = d * ATT_TILE + qry_i - key_i
    count = jnp.zeros(delta.shape, _f32)
    for window, dilation in DILATED_CONFIGS:
        hit = (delta >= 0) & (delta <= window) & (delta % dilation == 0)
        count = count + hit.astype(_f32)
    return jnp.where(count > 0, jnp.log2(jnp.maximum(count, 1.0)), NEG_INF)


def _dilated_kernel(qt_ref, k_ref, vt_ref, bias_ref, o_ref, w_ref, sd_ref, s_ref, m_ref, acc_ref):
    nq = qt_ref.shape[0]
    chunk = 2 * ATT_TILE
    masked = DIL_TILES_BACK + 1
    ones = jnp.ones((ONES_ROWS, ATT_TILE), _bf16)

    def head_vt(j, hd):
        return jnp.concatenate([vt_ref[j, hd * HEAD_DIM:(hd + 1) * HEAD_DIM, :], ones], axis=0)

    def q_tile(qi, carry):
        qt = qt_ref[qi]
        for hd in range(HEADS_PER_PAIR):
            w_ref[hd] = _head_rows(qt, hd)
        k_own = k_ref[qi]
        bias0 = bias_ref[0]
        for hd in range(HEADS_PER_PAIR):
            s = jnp.dot(k_own, w_ref[hd], preferred_element_type=_f32) + bias0
            sd_ref[hd] = s
            m_ref[hd] = jnp.max(s, axis=0, keepdims=True)

        lo = jnp.maximum(qi - DIL_TILES_BACK, 0)
        n_chunks = (qi - lo + 1) // 2

        def score_chunk(c, carry):
            j = lo + 2 * c
            keys = k_ref[pl.ds(j, 2)].reshape(chunk, PAIR_DIM)
            bias = jnp.concatenate(
                [bias_ref[qi - j], bias_ref[jnp.where(j + 1 < qi, qi - j - 1, masked)]], axis=0)
            for hd in range(HEADS_PER_PAIR):
                s = jnp.dot(keys, w_ref[hd], preferred_element_type=_f32) + bias
                s_ref[hd, c] = s
                m_ref[hd] = jnp.maximum(m_ref[hd], jnp.max(s, axis=0, keepdims=True))
            return carry

        lax.fori_loop(0, n_chunks, score_chunk, 0)

        for hd in range(HEADS_PER_PAIR):
            p = jnp.exp2(sd_ref[hd] - m_ref[hd])
            acc_ref[hd] = jnp.dot(head_vt(qi, hd), p.astype(_bf16), preferred_element_type=_f32)

        def value_chunk(c, carry):
            j = lo + 2 * c
            for hd in range(HEADS_PER_PAIR):
                p = jnp.exp2(s_ref[hd, c] - m_ref[hd])
                vtc = jnp.concatenate([head_vt(j, hd), head_vt(j + 1, hd)], axis=1)
                acc_ref[hd] += jnp.dot(vtc, p.astype(_bf16), preferred_element_type=_f32)
            return carry

        lax.fori_loop(0, n_chunks, value_chunk, 0)
        _write_out(o_ref, qi, [acc_ref[hd, :HEAD_DIM, :] / acc_ref[hd, HEAD_DIM:HEAD_DIM + 1, :]
                               for hd in range(HEADS_PER_PAIR)])
        return carry

    lax.fori_loop(0, nq, q_tile, 0)


def _dilated_attention(qt, k, vt, bias):
    bias_spec = pl.BlockSpec(bias.shape, lambda bi, pi: (0, 0, 0), pipeline_mode=pl.Buffered(1))
    scratch = [pltpu.VMEM((HEADS_PER_PAIR, PAIR_DIM, ATT_TILE), _bf16),
               pltpu.VMEM((HEADS_PER_PAIR, ATT_TILE, ATT_TILE), _f32),
               pltpu.VMEM((HEADS_PER_PAIR, DIL_TILES_BACK // 2, 2 * ATT_TILE, ATT_TILE), _f32),
               pltpu.VMEM((HEADS_PER_PAIR, 1, ATT_TILE), _f32),
               pltpu.VMEM((HEADS_PER_PAIR, HEAD_DIM + ONES_ROWS, ATT_TILE), _f32)]
    return _attention_call(_dilated_kernel, "dilated", qt, k, vt, [bias], [bias_spec], scratch)


def _post_kernel(x_ref, oa_ref, od_ref, ga_ref, gd_ref, wout_ref, gm_ref, wup_ref, wdown_ref,
                 gf_ref, o_ref, *, final_norm):
    da = oa_ref.shape[1]
    na = _rms_norm_rows(oa_ref[...].astype(_f32), ga_ref[...]).astype(_bf16)
    nd = _rms_norm_rows(od_ref[...].astype(_f32), gd_ref[...]).astype(_bf16)
    y = (x_ref[...]
         + jnp.dot(na, wout_ref[:da, :], preferred_element_type=_f32)
         + jnp.dot(nd, wout_ref[da:, :], preferred_element_type=_f32))
    h = _rms_norm_rows(y, gm_ref[...]).astype(_bf16)
    u = jnp.dot(h, wup_ref[...], preferred_element_type=_f32)
    u = jnp.square(jnp.maximum(u, 0.0)).astype(_bf16)
    y = y + jnp.dot(u, wdown_ref[...], preferred_element_type=_f32)
    if final_norm:
        y = _rms_norm_rows(y, gf_ref[...])
    o_ref[...] = y


def _post(x, oa, od, ga, gd, wout, gm, wup, wdown, gf, *, tm, final_norm):
    b, s, d = x.shape
    da = oa.shape[-1]
    dd = od.shape[-1]
    row = lambda bi, si: (bi, si, 0)
    const = lambda bi, si: (0, 0)
    whole = lambda a: pl.BlockSpec(a.shape, const, pipeline_mode=pl.Buffered(1))
    return pl.pallas_call(
        functools.partial(_post_kernel, final_norm=final_norm),
        grid=(b, s // tm),
        in_specs=[
            pl.BlockSpec((None, tm, d), row),
            pl.BlockSpec((None, tm, da), row),
            pl.BlockSpec((None, tm, dd), row),
            pl.BlockSpec((1, da), const),
            pl.BlockSpec((1, dd), const),
            whole(wout),
            pl.BlockSpec((1, d), const),
            whole(wup),
            whole(wdown),
            pl.BlockSpec((1, d), const),
        ],
        out_specs=pl.BlockSpec((None, tm, d), row),
        out_shape=jax.ShapeDtypeStruct((b, s, d), _f32),
        compiler_params=pltpu.CompilerParams(
            dimension_semantics=("parallel", "parallel"), vmem_limit_bytes=VMEM_LIMIT_BYTES),
        name="post",
    )(x, oa, od, ga, gd, wout, gm, wup, wdown, gf)


def _rope_tables(seq):
    inv = 1.0 / (ROPE_THETA ** (jnp.arange(0, HEAD_DIM, 2, dtype=_f32) / HEAD_DIM))
    ang = jnp.arange(seq, dtype=_f32)[:, None] * inv[None, :]
    cos, sin = jnp.cos(ang), jnp.sin(ang)
    reps = PAIR_DIM // (HEAD_DIM // 2)
    cos_t = jnp.tile(cos, (1, reps))
    sin_t = jnp.tile(jnp.concatenate([-sin, sin], axis=-1), (1, HEADS_PER_PAIR))
    return cos_t, sin_t


def kernel(x, attn_norm, w_in, moba_out_norm, dil_out_norm, w_out, mlp_norm, w_up, w_down, final_norm):
    depth = w_in.shape[0]
    seq = x.shape[1]
    cos_t, sin_t = _rope_tables(seq)
    bias = _dilated_bias_table()
    row = lambda v: v.reshape(1, -1)
    for l in range(depth):
        qta, ka, vta, qtd, kd, vtd = _inproj(
            x, row(attn_norm[l]), w_in[l].astype(_bf16), cos_t, sin_t, tm=512)
        out_a = _moba_attention(qta, ka, vta)
        out_d = _dilated_attention(qtd, kd, vtd, bias)
        x = _post(x, out_a, out_d, row(moba_out_norm[l]), row(dil_out_norm[l]),
                  w_out[l].astype(_bf16), row(mlp_norm[l]), w_up[l].astype(_bf16),
                  w_down[l].astype(_bf16), row(final_norm), tm=512, final_norm=(l == depth - 1))
    return x
```

```python
import functools
import math

import jax
import jax.numpy as jnp
from jax import lax
from jax.experimental import pallas as pl
from jax.experimental.pallas import tpu as pltpu

HEAD_DIM = 64
HEADS_PER_PAIR = 2
PAIR_DIM = HEADS_PER_PAIR * HEAD_DIM
MOBA_BLOCK = 256
MOBA_TOPK = 3
DILATED_CONFIGS = ((128, 1), (512, 4), (2048, 16))
ROPE_THETA = 10000.0
NORM_EPS = 1e-6
NEG_INF = -1e30
ATT_TILE = 256
ONES_ROWS = 16
DIL_TILES_BACK = max(w for w, _ in DILATED_CONFIGS) // ATT_TILE
VMEM_LIMIT_BYTES = 56 * 1024 * 1024

_f32 = jnp.float32
_bf16 = jnp.bfloat16


def _rms_norm_rows(x, g):
    return x * lax.rsqrt(jnp.mean(x * x, axis=-1, keepdims=True) + NORM_EPS) * g


def _inproj_kernel(x_ref, g_ref, w_ref, cos_ref, sin_ref,
                   qta_ref, ka_ref, vta_ref, qtd_ref, kd_ref, vtd_ref, *, n_pairs, scale):
    tm = x_ref.shape[0]
    nt = tm // ATT_TILE
    h = _rms_norm_rows(x_ref[...], g_ref[...]).astype(_bf16)
    proj = jnp.dot(h, w_ref[...], preferred_element_type=_f32)
    cos = cos_ref[...]
    sin = sin_ref[...]
    lane = lax.broadcasted_iota(jnp.int32, (tm, PAIR_DIM), 1)
    first_half = (lane % HEAD_DIM) < (HEAD_DIM // 2)

    def rope(c):
        partner = jnp.where(first_half,
                            pltpu.roll(c, PAIR_DIM - HEAD_DIM // 2, 1),
                            pltpu.roll(c, HEAD_DIM // 2, 1))
        return c * cos + partner * sin

    def chunk(group, p):
        c0 = (group * n_pairs + p) * PAIR_DIM
        return proj[:, c0:c0 + PAIR_DIM]

    def put_transposed(ref, p, c):
        for t in range(nt):
            ref[p, t] = c[t * ATT_TILE:(t + 1) * ATT_TILE, :].T.astype(_bf16)

    def put_rows(ref, p, c):
        for t in range(nt):
            ref[p, t] = c[t * ATT_TILE:(t + 1) * ATT_TILE, :].astype(_bf16)

    for p in range(n_pairs):
        put_transposed(qta_ref, p, rope(chunk(0, p)) * scale)
        put_rows(ka_ref, p, rope(chunk(1, p)))
        put_transposed(vta_ref, p, chunk(2, p))
        put_transposed(qtd_ref, p, rope(chunk(3, p)) * scale)
        put_rows(kd_ref, p, rope(chunk(4, p)))
        put_transposed(vtd_ref, p, chunk(5, p))


def _inproj(x, g, w, cos_t, sin_t, *, tm):
    b, s, d = x.shape
    n_pairs = w.shape[1] // (6 * PAIR_DIM)
    nq = s // ATT_TILE
    nt = tm // ATT_TILE
    t_shape = jax.ShapeDtypeStruct((b, n_pairs, nq, PAIR_DIM, ATT_TILE), _bf16)
    r_shape = jax.ShapeDtypeStruct((b, n_pairs, nq, ATT_TILE, PAIR_DIM), _bf16)
    t_spec = pl.BlockSpec((None, n_pairs, nt, PAIR_DIM, ATT_TILE), lambda bi, si: (bi, 0, si, 0, 0))
    r_spec = pl.BlockSpec((None, n_pairs, nt, ATT_TILE, PAIR_DIM), lambda bi, si: (bi, 0, si, 0, 0))
    return pl.pallas_call(
        functools.partial(_inproj_kernel, n_pairs=n_pairs, scale=HEAD_DIM ** -0.5 * math.log2(math.e)),
        grid=(b, s // tm),
        in_specs=[
            pl.BlockSpec((None, tm, d), lambda bi, si: (bi, si, 0)),
            pl.BlockSpec((1, d), lambda bi, si: (0, 0)),
            pl.BlockSpec(w.shape, lambda bi, si: (0, 0), pipeline_mode=pl.Buffered(1)),
            pl.BlockSpec((tm, PAIR_DIM), lambda bi, si: (si, 0)),
            pl.BlockSpec((tm, PAIR_DIM), lambda bi, si: (si, 0)),
        ],
        out_specs=[t_spec, r_spec, t_spec, t_spec, r_spec, t_spec],
        out_shape=[t_shape, r_shape, t_shape, t_shape, r_shape, t_shape],
        compiler_params=pltpu.CompilerParams(
            dimension_semantics=("parallel", "parallel"), vmem_limit_bytes=VMEM_LIMIT_BYTES),
        name="inproj",
    )(x, g, w, cos_t, sin_t)


def _head_rows(qt, head):
    row = lax.broadcasted_iota(jnp.int32, qt.shape, 0)
    mine = (row >= head * HEAD_DIM) & (row < (head + 1) * HEAD_DIM)
    return jnp.where(mine, qt, jnp.zeros_like(qt))


def _write_out(o_ref, qi, outs):
    o_t = jnp.concatenate(outs, axis=0)
    row0 = qi * ATT_TILE
    if not isinstance(row0, int):
        row0 = pl.multiple_of(row0, ATT_TILE)
    o_ref[pl.ds(row0, ATT_TILE), :] = o_t.T.astype(o_ref.dtype)


def _pair_pipeline(n_pairs, n_steps, prepare, value_first, score_step, value_step, finish):
    def stage(score_pair, value_pair, score_par):
        value_par = 1 - score_par
        if score_pair is not None:
            prepare(score_pair, score_par)
        if value_pair is not None:
            value_first(value_pair, value_par)
        for step in range(n_steps):
            if value_pair is not None:
                value_step(value_pair, value_par, step)
            if score_pair is not None:
                score_step(score_pair, score_par, step)
        if value_pair is not None:
            finish(value_pair)

    assert n_pairs >= 2 and n_pairs % 2 == 0
    stage(0, None, 0)

    def two_pairs(u, carry):
        stage(2 * u + 1, 2 * u, 1)
        stage(2 * u + 2, 2 * u + 1, 0)
        return carry

    lax.fori_loop(0, n_pairs // 2 - 1, two_pairs, 0)
    stage(n_pairs - 1, n_pairs - 2, 1)
    stage(None, n_pairs - 1, 0)


def _attention_call(kernel, name, qt, k, vt, extra_inputs, extra_specs, scratch):
    b, n_pairs, nq = qt.shape[:3]
    s = nq * ATT_TILE
    t_spec = pl.BlockSpec((None, None, nq, PAIR_DIM, ATT_TILE), lambda bi, pi: (bi, pi, 0, 0, 0))
    r_spec = pl.BlockSpec((None, None, nq, ATT_TILE, PAIR_DIM), lambda bi, pi: (bi, pi, 0, 0, 0))
    return pl.pallas_call(
        kernel,
        grid=(b, n_pairs),
        in_specs=[t_spec, r_spec, t_spec] + extra_specs,
        out_specs=pl.BlockSpec((None, s, PAIR_DIM), lambda bi, pi: (bi, 0, pi)),
        out_shape=jax.ShapeDtypeStruct((b, s, n_pairs * PAIR_DIM), _bf16),
        scratch_shapes=scratch,
        compiler_params=pltpu.CompilerParams(
            dimension_semantics=("parallel", "parallel"), vmem_limit_bytes=VMEM_LIMIT_BYTES),
        name=name,
    )(qt, k, vt, *extra_inputs)


def _moba_kernel(qt_ref, k_ref, vt_ref, aux_ref, o_ref,
                 kmean_ref, w_ref, sd_ref, s_ref, m_ref, acc_ref):
    nq = qt_ref.shape[0]
    n_steps = nq // 2
    chunk = 2 * ATT_TILE

    for j in range(nq):
        kmean_ref[j:j + 1, :] = jnp.mean(k_ref[j].astype(_f32), axis=0, keepdims=True)
    km = kmean_ref[...]
    km_hi = km.astype(_bf16)
    km_lo = (km - km_hi.astype(_f32)).astype(_bf16)
    for par in range(2):
        for slot in range(2):
            for hd in range(HEADS_PER_PAIR):
                w_ref[par, slot, hd, PAIR_DIM + nq:, :] = jnp.zeros((PAIR_DIM - nq, ATT_TILE), _bf16)

    key_i = lax.broadcasted_iota(jnp.int32, (ATT_TILE, ATT_TILE), 0)
    qry_i = lax.broadcasted_iota(jnp.int32, (ATT_TILE, ATT_TILE), 1)
    causal = key_i <= qry_i
    blk = lax.broadcasted_iota(jnp.int32, (nq, ATT_TILE), 0)
    ones = jnp.ones((ONES_ROWS, ATT_TILE), _bf16)

    def head_vt(j, hd):
        return jnp.concatenate([vt_ref[j, hd * HEAD_DIM:(hd + 1) * HEAD_DIM, :], ones], axis=0)

    def tile_of(pair, slot):
        return pair if slot == 0 else nq - 1 - pair

    def plan(pair, step):
        n_first = (pair + 1) // 2
        first = step < n_first
        return jnp.where(first, 0, 1), jnp.where(first, step, step - n_first)

    def prepare(pair, par):
        for slot in range(2):
            qi = tile_of(pair, slot)
            qt = qt_ref[qi]
            past = blk < qi
            for hd in range(HEADS_PER_PAIR):
                qth = _head_rows(qt, hd)
                gate = (jnp.dot(km_hi, qth, preferred_element_type=_f32)
                        + jnp.dot(km_lo, qth, preferred_element_type=_f32))
                gate = jnp.where(past, gate, NEG_INF)
                rank = jnp.zeros(gate.shape, _f32)
                for i in range(nq):
                    gi = gate[i:i + 1, :]
                    rank = rank + jnp.where(blk > i, jnp.where(gi >= gate, 1.0, 0.0),
                                            jnp.where(gi > gate, 1.0, 0.0))
                sel = jnp.where(past, jnp.where(rank < MOBA_TOPK, 0.0, NEG_INF), NEG_INF)
                w_ref[par, slot, hd, :PAIR_DIM, :] = qth
                w_ref[par, slot, hd, PAIR_DIM:PAIR_DIM + nq, :] = sel.astype(_bf16)
                s = jnp.dot(k_ref[qi], qth, preferred_element_type=_f32)
                s = jnp.where(causal, s, NEG_INF)
                sd_ref[par, slot, hd] = s
                m_ref[par, slot, hd] = jnp.max(s, axis=0, keepdims=True)

    def score_step(pair, par, step):
        slot, c = plan(pair, step)
        keys = k_ref[pl.ds(2 * c, 2)].reshape(chunk, PAIR_DIM)
        lhs = jnp.concatenate([keys, aux_ref[c]], axis=1)
        for hd in range(HEADS_PER_PAIR):
            s = jnp.dot(lhs, w_ref[par, slot, hd], preferred_element_type=_f32)
            s_ref[par, step, hd] = s
            m_ref[par, slot, hd] = jnp.maximum(m_ref[par, slot, hd], jnp.max(s, axis=0, keepdims=True))

    def value_first(pair, par):
        for slot in range(2):
            qi = tile_of(pair, slot)
            for hd in range(HEADS_PER_PAIR):
                p = jnp.exp2(sd_ref[par, slot, hd] - m_ref[par, slot, hd])
                acc_ref[slot, hd] = jnp.dot(head_vt(qi, hd), p.astype(_bf16), preferred_element_type=_f32)

    def value_step(pair, par, step):
        slot, c = plan(pair, step)
        for hd in range(HEADS_PER_PAIR):
            p = jnp.exp2(s_ref[par, step, hd] - m_ref[par, slot, hd])
            vtc = jnp.concatenate([head_vt(2 * c, hd), head_vt(2 * c + 1, hd)], axis=1)
            acc_ref[slot, hd] += jnp.dot(vtc, p.astype(_bf16), preferred_element_type=_f32)

    def finish(pair):
        for slot in range(2):
            _write_out(o_ref, tile_of(pair, slot),
                       [acc_ref[slot, hd, :HEAD_DIM, :] / acc_ref[slot, hd, HEAD_DIM:HEAD_DIM + 1, :]
                        for hd in range(HEADS_PER_PAIR)])

    _pair_pipeline(nq // 2, n_steps, prepare, value_first, score_step, value_step, finish)


def _moba_block_onehot(nq):
    c = jnp.arange(nq // 2, dtype=jnp.int32)[:, None, None]
    r = jnp.arange(2 * ATT_TILE, dtype=jnp.int32)[None, :, None]
    lane = jnp.arange(PAIR_DIM, dtype=jnp.int32)[None, None, :]
    return (lane == 2 * c + r // ATT_TILE).astype(_bf16)


def _moba_attention(qt, k, vt):
    nq = qt.shape[2]
    aux = _moba_block_onehot(nq)
    aux_spec = pl.BlockSpec(aux.shape, lambda bi, pi: (0, 0, 0), pipeline_mode=pl.Buffered(1))
    scratch = [pltpu.VMEM((nq, PAIR_DIM), _f32),
               pltpu.VMEM((2, 2, HEADS_PER_PAIR, 2 * PAIR_DIM, ATT_TILE), _bf16),
               pltpu.VMEM((2, 2, HEADS_PER_PAIR, ATT_TILE, ATT_TILE), _f32),
               pltpu.VMEM((2, nq // 2, HEADS_PER_PAIR, 2 * ATT_TILE, ATT_TILE), _f32),
               pltpu.VMEM((2, 2, HEADS_PER_PAIR, 1, ATT_TILE), _f32),
               pltpu.VMEM((2, HEADS_PER_PAIR, HEAD_DIM + ONES_ROWS, ATT_TILE), _f32)]
    return _attention_call(_moba_kernel, "moba", qt, k, vt, [aux], [aux_spec], scratch)


def _dilated_bias_table():
    d = jnp.arange(DIL_TILES_BACK + 2, dtype=jnp.int32)[:, None, None]
    key_i = jnp.arange(ATT_TILE, dtype=jnp.int32)[None, :, None]
    qry_i = jnp.arange(ATT_TILE, dtype=jnp.int32)[None, None, :]
    delta = d * ATT_TILE + qry_i - key_i
    count = jnp.zeros(delta.shape, _f32)
    for window, dilation in DILATED_CONFIGS:
        hit = (delta >= 0) & (delta <= window) & (delta % dilation == 0)
        count = count + hit.astype(_f32)
    return jnp.where(count > 0, jnp.log2(jnp.maximum(count, 1.0)), NEG_INF)


def _dilated_kernel(qt_ref, k_ref, vt_ref, bias_ref, o_ref, w_ref, sd_ref, s_ref, m_ref, acc_ref):
    nq = qt_ref.shape[0]
    n_steps = DIL_TILES_BACK
    chunk = 2 * ATT_TILE
    masked = DIL_TILES_BACK + 1
    ones = jnp.ones((ONES_ROWS, ATT_TILE), _bf16)

    def head_vt(j, hd):
        return jnp.concatenate([vt_ref[j, hd * HEAD_DIM:(hd + 1) * HEAD_DIM, :], ones], axis=0)

    def tile_of(pair, slot):
        return pair if slot == 0 else nq - 1 - pair

    def n_chunks(qi):
        return (jnp.minimum(qi, DIL_TILES_BACK) + 1) // 2

    def plan(pair, step):
        t_first, t_second = tile_of(pair, 0), tile_of(pair, 1)
        n_first = n_chunks(t_first)
        first = step < n_first
        c = jnp.where(first, step, step - n_first)
        live = first | (c < n_chunks(t_second))
        qi = jnp.where(first, t_first, t_second)
        j = jnp.where(live, jnp.maximum(qi - DIL_TILES_BACK, 0) + 2 * c, 0)
        d0 = jnp.where(live, qi - j, masked)
        d1 = jnp.where(live & (j + 1 < qi), qi - j - 1, masked)
        return jnp.where(first, 0, 1), j, d0, d1

    def prepare(pair, par):
        bias0 = bias_ref[0]
        for slot in range(2):
            qi = tile_of(pair, slot)
            qt = qt_ref[qi]
            for hd in range(HEADS_PER_PAIR):
                qth = _head_rows(qt, hd)
                w_ref[par, slot, hd] = qth
                s = jnp.dot(k_ref[qi], qth, preferred_element_type=_f32) + bias0
                sd_ref[par, slot, hd] = s
                m_ref[par, slot, hd] = jnp.max(s, axis=0, keepdims=True)

    def score_step(pair, par, step):
        slot, j, d0, d1 = plan(pair, step)
        keys = k_ref[pl.ds(j, 2)].reshape(chunk, PAIR_DIM)
        bias = jnp.concatenate([bias_ref[d0], bias_ref[d1]], axis=0)
        for hd in range(HEADS_PER_PAIR):
            s = jnp.dot(keys, w_ref[par, slot, hd], preferred_element_type=_f32) + bias
            s_ref[par, step, hd] = s
            m_ref[par, slot, hd] = jnp.maximum(m_ref[par, slot, hd], jnp.max(s, axis=0, keepdims=True))

    def value_first(pair, par):
        for slot in range(2):
            qi = tile_of(pair, slot)
            for hd in range(HEADS_PER_PAIR):
                p = jnp.exp2(sd_ref[par, slot, hd] - m_ref[par, slot, hd])
                acc_ref[slot, hd] = jnp.dot(head_vt(qi, hd), p.astype(_bf16), preferred_element_type=_f32)

    def value_step(pair, par, step):
        slot, j, _, _ = plan(pair, step)
        for hd in range(HEADS_PER_PAIR):
            p = jnp.exp2(s_ref[par, step, hd] - m_ref[par, slot, hd])
            vtc = jnp.concatenate([head_vt(j, hd), head_vt(j + 1, hd)], axis=1)
            acc_ref[slot, hd] += jnp.dot(vtc, p.astype(_bf16), preferred_element_type=_f32)

    def finish(pair):
        for slot in range(2):
            _write_out(o_ref, tile_of(pair, slot),
                       [acc_ref[slot, hd, :HEAD_DIM, :] / acc_ref[slot, hd, HEAD_DIM:HEAD_DIM + 1, :]
                        for hd in range(HEADS_PER_PAIR)])

    _pair_pipeline(nq // 2, n_steps, prepare, value_first, score_step, value_step, finish)


def _dilated_attention(qt, k, vt, bias):
    bias_spec = pl.BlockSpec(bias.shape, lambda bi, pi: (0, 0, 0), pipeline_mode=pl.Buffered(1))
    scratch = [pltpu.VMEM((2, 2, HEADS_PER_PAIR, PAIR_DIM, ATT_TILE), _bf16),
               pltpu.VMEM((2, 2, HEADS_PER_PAIR, ATT_TILE, ATT_TILE), _f32),
               pltpu.VMEM((2, DIL_TILES_BACK, HEADS_PER_PAIR, 2 * ATT_TILE, ATT_TILE), _f32),
               pltpu.VMEM((2, 2, HEADS_PER_PAIR, 1, ATT_TILE), _f32),
               pltpu.VMEM((2, HEADS_PER_PAIR, HEAD_DIM + ONES_ROWS, ATT_TILE), _f32)]
    return _attention_call(_dilated_kernel, "dilated", qt, k, vt, [bias], [bias_spec], scratch)


def _post_kernel(x_ref, oa_ref, od_ref, ga_ref, gd_ref, wout_ref, gm_ref, wup_ref, wdown_ref,
                 gf_ref, o_ref, *, final_norm):
    da = oa_ref.shape[1]
    na = _rms_norm_rows(oa_ref[...].astype(_f32), ga_ref[...]).astype(_bf16)
    nd = _rms_norm_rows(od_ref[...].astype(_f32), gd_ref[...]).astype(_bf16)
    y = (x_ref[...]
         + jnp.dot(na, wout_ref[:da, :], preferred_element_type=_f32)
         + jnp.dot(nd, wout_ref[da:, :], preferred_element_type=_f32))
    h = _rms_norm_rows(y, gm_ref[...]).astype(_bf16)
    u = jnp.dot(h, wup_ref[...], preferred_element_type=_f32)
    u = jnp.square(jnp.maximum(u, 0.0)).astype(_bf16)
    y = y + jnp.dot(u, wdown_ref[...], preferred_element_type=_f32)
    if final_norm:
        y = _rms_norm_rows(y, gf_ref[...])
    o_ref[...] = y


def _post(x, oa, od, ga, gd, wout, gm, wup, wdown, gf, *, tm, final_norm):
    b, s, d = x.shape
    da = oa.shape[-1]
    dd = od.shape[-1]
    row = lambda bi, si: (bi, si, 0)
    const = lambda bi, si: (0, 0)
    whole = lambda a: pl.BlockSpec(a.shape, const, pipeline_mode=pl.Buffered(1))
    return pl.pallas_call(
        functools.partial(_post_kernel, final_norm=final_norm),
        grid=(b, s // tm),
        in_specs=[
            pl.BlockSpec((None, tm, d), row),
            pl.BlockSpec((None, tm, da), row),
            pl.BlockSpec((None, tm, dd), row),
            pl.BlockSpec((1, da), const),
            pl.BlockSpec((1, dd), const),
            whole(wout),
            pl.BlockSpec((1, d), const),
            whole(wup),
            whole(wdown),
            pl.BlockSpec((1, d), const),
        ],
        out_specs=pl.BlockSpec((None, tm, d), row),
        out_shape=jax.ShapeDtypeStruct((b, s, d), _f32),
        compiler_params=pltpu.CompilerParams(
            dimension_semantics=("parallel", "parallel"), vmem_limit_bytes=VMEM_LIMIT_BYTES),
        name="post",
    )(x, oa, od, ga, gd, wout, gm, wup, wdown, gf)


def _rope_tables(seq):
    inv = 1.0 / (ROPE_THETA ** (jnp.arange(0, HEAD_DIM, 2, dtype=_f32) / HEAD_DIM))
    ang = jnp.arange(seq, dtype=_f32)[:, None] * inv[None, :]
    cos, sin = jnp.cos(ang), jnp.sin(ang)
    reps = PAIR_DIM // (HEAD_DIM // 2)
    cos_t = jnp.tile(cos, (1, reps))
    sin_t = jnp.tile(jnp.concatenate([-sin, sin], axis=-1), (1, HEADS_PER_PAIR))
    return cos_t, sin_t


def kernel(x, attn_norm, w_in, moba_out_norm, dil_out_norm, w_out, mlp_norm, w_up, w_down, final_norm):
    depth = w_in.shape[0]
    seq = x.shape[1]
    cos_t, sin_t = _rope_tables(seq)
    bias = _dilated_bias_table()
    row = lambda v: v.reshape(1, -1)
    for l in range(depth):
        qta, ka, vta, qtd, kd, vtd = _inproj(
            x, row(attn_norm[l]), w_in[l].astype(_bf16), cos_t, sin_t, tm=512)
        out_a = _moba_attention(qta, ka, vta)
        out_d = _dilated_attention(qtd, kd, vtd, bias)
        x = _post(x, out_a, out_d, row(moba_out_norm[l]), row(dil_out_norm[l]),
                  w_out[l].astype(_bf16), row(mlp_norm[l]), w_up[l].astype(_bf16),
                  w_down[l].astype(_bf16), row(final_norm), tm=512, final_norm=(l == depth - 1))
    return x
```

```python
import functools
import math

import jax
import jax.numpy as jnp
from jax import lax
from jax.experimental import pallas as pl
from jax.experimental.pallas import tpu as pltpu

HEAD_DIM = 64
HEADS_PER_PAIR = 2
PAIR_DIM = HEADS_PER_PAIR * HEAD_DIM
MOBA_BLOCK = 256
MOBA_TOPK = 3
DILATED_CONFIGS = ((128, 1), (512, 4), (2048, 16))
ROPE_THETA = 10000.0
NORM_EPS = 1e-6
NEG_INF = -1e30
ATT_TILE = 256
ONES_ROWS = 16
VMEM_LIMIT_BYTES = 56 * 1024 * 1024

_f32 = jnp.float32
_bf16 = jnp.bfloat16


def _rms_norm_rows(x, g):
    return x * lax.rsqrt(jnp.mean(x * x, axis=-1, keepdims=True) + NORM_EPS) * g


def _inproj_kernel(x_ref, g_ref, w_ref, cos_ref, sin_ref,
                   qta_ref, ka_ref, vta_ref, qtd_ref, kd_ref, vtd_ref, qrd_ref, vrd_ref, *, n_pairs, scale):
    tm = x_ref.shape[0]
    nt = tm // ATT_TILE
    h = _rms_norm_rows(x_ref[...], g_ref[...]).astype(_bf16)
    proj = jnp.dot(h, w_ref[...], preferred_element_type=_f32)
    cos = cos_ref[...]
    sin = sin_ref[...]
    lane = lax.broadcasted_iota(jnp.int32, (tm, PAIR_DIM), 1)
    first_half = (lane % HEAD_DIM) < (HEAD_DIM // 2)

    def rope(c):
        partner = jnp.where(first_half,
                            pltpu.roll(c, PAIR_DIM - HEAD_DIM // 2, 1),
                            pltpu.roll(c, HEAD_DIM // 2, 1))
        return c * cos + partner * sin

    def chunk(group, p):
        c0 = (group * n_pairs + p) * PAIR_DIM
        return proj[:, c0:c0 + PAIR_DIM]

    def put_transposed(ref, p, c):
        for t in range(nt):
            ref[p, t] = c[t * ATT_TILE:(t + 1) * ATT_TILE, :].T.astype(_bf16)

    def put_rows(ref, p, c):
        for t in range(nt):
            ref[p, t] = c[t * ATT_TILE:(t + 1) * ATT_TILE, :].astype(_bf16)

    for p in range(n_pairs):
        put_transposed(qta_ref, p, rope(chunk(0, p)) * scale)
        put_rows(ka_ref, p, rope(chunk(1, p)))
        put_transposed(vta_ref, p, chunk(2, p))
        qd = rope(chunk(3, p)) * scale
        put_transposed(qtd_ref, p, qd)
        put_rows(qrd_ref, p, qd)
        put_rows(kd_ref, p, rope(chunk(4, p)))
        put_transposed(vtd_ref, p, chunk(5, p))
        put_rows(vrd_ref, p, chunk(5, p))


def _inproj(x, g, w, cos_t, sin_t, *, tm):
    b, s, d = x.shape
    n_pairs = w.shape[1] // (6 * PAIR_DIM)
    nq = s // ATT_TILE
    nt = tm // ATT_TILE
    t_shape = jax.ShapeDtypeStruct((b, n_pairs, nq, PAIR_DIM, ATT_TILE), _bf16)
    r_shape = jax.ShapeDtypeStruct((b, n_pairs, nq, ATT_TILE, PAIR_DIM), _bf16)
    t_spec = pl.BlockSpec((None, n_pairs, nt, PAIR_DIM, ATT_TILE), lambda bi, si: (bi, 0, si, 0, 0))
    r_spec = pl.BlockSpec((None, n_pairs, nt, ATT_TILE, PAIR_DIM), lambda bi, si: (bi, 0, si, 0, 0))
    return pl.pallas_call(
        functools.partial(_inproj_kernel, n_pairs=n_pairs, scale=HEAD_DIM ** -0.5 * math.log2(math.e)),
        grid=(b, s // tm),
        in_specs=[
            pl.BlockSpec((None, tm, d), lambda bi, si: (bi, si, 0)),
            pl.BlockSpec((1, d), lambda bi, si: (0, 0)),
            pl.BlockSpec(w.shape, lambda bi, si: (0, 0), pipeline_mode=pl.Buffered(1)),
            pl.BlockSpec((tm, PAIR_DIM), lambda bi, si: (si, 0)),
            pl.BlockSpec((tm, PAIR_DIM), lambda bi, si: (si, 0)),
        ],
        out_specs=[t_spec, r_spec, t_spec, t_spec, r_spec, t_spec, r_spec, r_spec],
        out_shape=[t_shape, r_shape, t_shape, t_shape, r_shape, t_shape, r_shape, r_shape],
        compiler_params=pltpu.CompilerParams(
            dimension_semantics=("parallel", "parallel"), vmem_limit_bytes=VMEM_LIMIT_BYTES),
        name="inproj",
    )(x, g, w, cos_t, sin_t)


def _head_rows(qt, head):
    row = lax.broadcasted_iota(jnp.int32, qt.shape, 0)
    mine = (row >= head * HEAD_DIM) & (row < (head + 1) * HEAD_DIM)
    return jnp.where(mine, qt, jnp.zeros_like(qt))


def _write_out(o_ref, qi, outs):
    o_t = jnp.concatenate(outs, axis=0)
    row0 = qi * ATT_TILE
    if not isinstance(row0, int):
        row0 = pl.multiple_of(row0, ATT_TILE)
    o_ref[pl.ds(row0, ATT_TILE), :] = o_t.T.astype(o_ref.dtype)


def _pair_pipeline(n_pairs, n_steps, prepare, value_first, score_step, value_step, finish):
    def stage(score_pair, value_pair, score_par):
        value_par = 1 - score_par
        if score_pair is not None:
            prepare(score_pair, score_par)
        if value_pair is not None:
            value_first(value_pair, value_par)
        for step in range(n_steps):
            if value_pair is not None:
                value_step(value_pair, value_par, step)
            if score_pair is not None:
                score_step(score_pair, score_par, step)
        if value_pair is not None:
            finish(value_pair)

    assert n_pairs >= 2 and n_pairs % 2 == 0
    stage(0, None, 0)

    def two_pairs(u, carry):
        stage(2 * u + 1, 2 * u, 1)
        stage(2 * u + 2, 2 * u + 1, 0)
        return carry

    lax.fori_loop(0, n_pairs // 2 - 1, two_pairs, 0)
    stage(n_pairs - 1, n_pairs - 2, 1)
    stage(None, n_pairs - 1, 0)


def _attention_call(kernel, name, qt, k, vt, extra_inputs, extra_specs, scratch):
    b, n_pairs, nq = qt.shape[:3]
    s = nq * ATT_TILE
    t_spec = pl.BlockSpec((None, None, nq, PAIR_DIM, ATT_TILE), lambda bi, pi: (bi, pi, 0, 0, 0))
    r_spec = pl.BlockSpec((None, None, nq, ATT_TILE, PAIR_DIM), lambda bi, pi: (bi, pi, 0, 0, 0))
    return pl.pallas_call(
        kernel,
        grid=(b, n_pairs),
        in_specs=[t_spec, r_spec, t_spec] + extra_specs,
        out_specs=pl.BlockSpec((None, s, PAIR_DIM), lambda bi, pi: (bi, 0, pi)),
        out_shape=jax.ShapeDtypeStruct((b, s, n_pairs * PAIR_DIM), _bf16),
        scratch_shapes=scratch,
        compiler_params=pltpu.CompilerParams(
            dimension_semantics=("parallel", "parallel"), vmem_limit_bytes=VMEM_LIMIT_BYTES),
        name=name,
    )(qt, k, vt, *extra_inputs)


def _moba_kernel(qt_ref, k_ref, vt_ref, aux_ref, o_ref,
                 kmean_ref, w_ref, sd_ref, s_ref, m_ref, acc_ref):
    nq = qt_ref.shape[0]
    n_steps = nq // 2
    chunk = 2 * ATT_TILE

    for j in range(nq):
        kmean_ref[j:j + 1, :] = jnp.mean(k_ref[j].astype(_f32), axis=0, keepdims=True)
    km = kmean_ref[...]
    km_hi = km.astype(_bf16)
    km_lo = (km - km_hi.astype(_f32)).astype(_bf16)
    for par in range(2):
        for slot in range(2):
            for hd in range(HEADS_PER_PAIR):
                w_ref[par, slot, hd, PAIR_DIM + nq:, :] = jnp.zeros((PAIR_DIM - nq, ATT_TILE), _bf16)

    key_i = lax.broadcasted_iota(jnp.int32, (ATT_TILE, ATT_TILE), 0)
    qry_i = lax.broadcasted_iota(jnp.int32, (ATT_TILE, ATT_TILE), 1)
    causal = key_i <= qry_i
    blk = lax.broadcasted_iota(jnp.int32, (nq, ATT_TILE), 0)
    ones = jnp.ones((ONES_ROWS, ATT_TILE), _bf16)

    def head_vt(j, hd):
        return jnp.concatenate([vt_ref[j, hd * HEAD_DIM:(hd + 1) * HEAD_DIM, :], ones], axis=0)

    def tile_of(pair, slot):
        return pair if slot == 0 else nq - 1 - pair

    def plan(pair, step):
        n_first = (pair + 1) // 2
        first = step < n_first
        return jnp.where(first, 0, 1), jnp.where(first, step, step - n_first)

    def prepare(pair, par):
        for slot in range(2):
            qi = tile_of(pair, slot)
            qt = qt_ref[qi]
            past = blk < qi
            for hd in range(HEADS_PER_PAIR):
                qth = _head_rows(qt, hd)
                gate = (jnp.dot(km_hi, qth, preferred_element_type=_f32)
                        + jnp.dot(km_lo, qth, preferred_element_type=_f32))
                gate = jnp.where(past, gate, NEG_INF)
                rank = jnp.zeros(gate.shape, _f32)
                for i in range(nq):
                    gi = gate[i:i + 1, :]
                    rank = rank + jnp.where(blk > i, jnp.where(gi >= gate, 1.0, 0.0),
                                            jnp.where(gi > gate, 1.0, 0.0))
                sel = jnp.where(past, jnp.where(rank < MOBA_TOPK, 0.0, NEG_INF), NEG_INF)
                w_ref[par, slot, hd, :PAIR_DIM, :] = qth
                w_ref[par, slot, hd, PAIR_DIM:PAIR_DIM + nq, :] = sel.astype(_bf16)
                s = jnp.dot(k_ref[qi], qth, preferred_element_type=_f32)
                s = jnp.where(causal, s, NEG_INF)
                sd_ref[par, slot, hd] = s
                m_ref[par, slot, hd] = jnp.max(s, axis=0, keepdims=True)

    def score_step(pair, par, step):
        slot, c = plan(pair, step)
        keys = k_ref[pl.ds(2 * c, 2)].reshape(chunk, PAIR_DIM)
        lhs = jnp.concatenate([keys, aux_ref[c]], axis=1)
        for hd in range(HEADS_PER_PAIR):
            s = jnp.dot(lhs, w_ref[par, slot, hd], preferred_element_type=_f32)
            s_ref[par, step, hd] = s
            m_ref[par, slot, hd] = jnp.maximum(m_ref[par, slot, hd], jnp.max(s, axis=0, keepdims=True))

    def value_first(pair, par):
        for slot in range(2):
            qi = tile_of(pair, slot)
            for hd in range(HEADS_PER_PAIR):
                p = jnp.exp2(sd_ref[par, slot, hd] - m_ref[par, slot, hd])
                acc_ref[slot, hd] = jnp.dot(head_vt(qi, hd), p.astype(_bf16), preferred_element_type=_f32)

    def value_step(pair, par, step):
        slot, c = plan(pair, step)
        for hd in range(HEADS_PER_PAIR):
            p = jnp.exp2(s_ref[par, step, hd] - m_ref[par, slot, hd])
            vtc = jnp.concatenate([head_vt(2 * c, hd), head_vt(2 * c + 1, hd)], axis=1)
            acc_ref[slot, hd] += jnp.dot(vtc, p.astype(_bf16), preferred_element_type=_f32)

    def finish(pair):
        for slot in range(2):
            _write_out(o_ref, tile_of(pair, slot),
                       [acc_ref[slot, hd, :HEAD_DIM, :] / acc_ref[slot, hd, HEAD_DIM:HEAD_DIM + 1, :]
                        for hd in range(HEADS_PER_PAIR)])

    _pair_pipeline(nq // 2, n_steps, prepare, value_first, score_step, value_step, finish)


def _moba_block_onehot(nq):
    c = jnp.arange(nq // 2, dtype=jnp.int32)[:, None, None]
    r = jnp.arange(2 * ATT_TILE, dtype=jnp.int32)[None, :, None]
    lane = jnp.arange(PAIR_DIM, dtype=jnp.int32)[None, None, :]
    return (lane == 2 * c + r // ATT_TILE).astype(_bf16)


def _moba_attention(qt, k, vt):
    nq = qt.shape[2]
    aux = _moba_block_onehot(nq)
    aux_spec = pl.BlockSpec(aux.shape, lambda bi, pi: (0, 0, 0), pipeline_mode=pl.Buffered(1))
    scratch = [pltpu.VMEM((nq, PAIR_DIM), _f32),
               pltpu.VMEM((2, 2, HEADS_PER_PAIR, 2 * PAIR_DIM, ATT_TILE), _bf16),
               pltpu.VMEM((2, 2, HEADS_PER_PAIR, ATT_TILE, ATT_TILE), _f32),
               pltpu.VMEM((2, nq // 2, HEADS_PER_PAIR, 2 * ATT_TILE, ATT_TILE), _f32),
               pltpu.VMEM((2, 2, HEADS_PER_PAIR, 1, ATT_TILE), _f32),
               pltpu.VMEM((2, HEADS_PER_PAIR, HEAD_DIM + ONES_ROWS, ATT_TILE), _f32)]
    return _attention_call(_moba_kernel, "moba", qt, k, vt, [aux], [aux_spec], scratch)


DENSE_CONFIGS = DILATED_CONFIGS[:2]
SPARSE_WINDOW, SPARSE_DILATION = DILATED_CONFIGS[2]
DENSE_TILES = max(w for w, _ in DENSE_CONFIGS) // ATT_TILE + 1


def _dense_bias_table():
    v = jnp.arange(DENSE_TILES, dtype=jnp.int32)[:, None, None]
    row = jnp.arange(DENSE_TILES * ATT_TILE, dtype=jnp.int32)[None, :, None]
    qry = jnp.arange(ATT_TILE, dtype=jnp.int32)[None, None, :]
    delta = v * ATT_TILE + qry - row
    count = jnp.zeros(delta.shape, _f32)
    for window, dilation in DENSE_CONFIGS:
        hit = (delta >= 0) & (delta <= window) & (delta % dilation == 0)
        count = count + hit.astype(_f32)
    return jnp.where(count > 0, jnp.log2(jnp.maximum(count, 1.0)), NEG_INF)


def _class_band_table():
    qry = jnp.arange(ATT_TILE, dtype=jnp.int32)[:, None]
    key = jnp.arange(ATT_TILE, dtype=jnp.int32)[None, :]
    delta = qry - key
    return jnp.where((delta >= 0) & (delta <= SPARSE_WINDOW // SPARSE_DILATION), 0.0, NEG_INF).astype(_f32)


def _dil_class_kernel(q_ref, k_ref, v_ref, band_ref, o_ref, lse_ref):
    n_class = q_ref.shape[1] // PAIR_DIM
    lane = lax.broadcasted_iota(jnp.int32, (ATT_TILE, PAIR_DIM), 1)
    band = band_ref[...]
    for r in range(n_class):
        cols = slice(r * PAIR_DIM, (r + 1) * PAIR_DIM)
        q, k, v = q_ref[:, cols], k_ref[:, cols], v_ref[:, cols]
        outs, lses = [], []
        for hd in range(HEADS_PER_PAIR):
            mine = (lane >= hd * HEAD_DIM) & (lane < (hd + 1) * HEAD_DIM)
            qh = jnp.where(mine, q, jnp.zeros_like(q))
            s = lax.dot_general(qh, k, (((1,), (1,)), ((), ())), preferred_element_type=_f32) + band
            m = jnp.max(s, axis=1, keepdims=True)
            p = jnp.exp2(s - m)
            l = jnp.sum(p, axis=1, keepdims=True)
            outs.append(jnp.dot(p.astype(_bf16), v, preferred_element_type=_f32) / l)
            lses.append(jnp.broadcast_to(m + jnp.log2(l), (ATT_TILE, PAIR_DIM)))
        first = lane < HEAD_DIM
        o_ref[:, cols] = jnp.where(first, outs[0], outs[1]).astype(o_ref.dtype)
        lse_ref[:, cols] = jnp.where(first, lses[0], lses[1])


def _dil_class_attention(q_rows, k_rows, v_rows, band):
    b, n_pairs, nq = q_rows.shape[:3]
    s = nq * ATT_TILE
    assert s // SPARSE_DILATION == ATT_TILE, "one query tile per residue class"
    wide = SPARSE_DILATION * PAIR_DIM
    view = lambda a: a.reshape(b, n_pairs, s // SPARSE_DILATION, wide)
    spec = pl.BlockSpec((None, None, s // SPARSE_DILATION, wide), lambda bi, pi: (bi, pi, 0, 0))
    o, lse = pl.pallas_call(
        _dil_class_kernel,
        grid=(b, n_pairs),
        in_specs=[spec, spec, spec, pl.BlockSpec(band.shape, lambda bi, pi: (0, 0))],
        out_specs=[spec, spec],
        out_shape=[jax.ShapeDtypeStruct((b, n_pairs, s // SPARSE_DILATION, wide), _bf16),
                   jax.ShapeDtypeStruct((b, n_pairs, s // SPARSE_DILATION, wide), _f32)],
        compiler_params=pltpu.CompilerParams(
            dimension_semantics=("parallel", "parallel"), vmem_limit_bytes=VMEM_LIMIT_BYTES),
        name="dilclass",
    )(view(q_rows), view(k_rows), view(v_rows), band)
    tiles = lambda a: a.reshape(b, n_pairs, nq, ATT_TILE, PAIR_DIM)
    return tiles(o), tiles(lse)


def _dilated_kernel(qt_ref, k_ref, vt_ref, bias_ref, o3_ref, lse3_ref, o_ref, s_ref, m_ref):
    nq = qt_ref.shape[0]
    ones = jnp.ones((ONES_ROWS, DENSE_TILES * ATT_TILE), _bf16)

    def first_tile(qi):
        return jnp.maximum(qi - (DENSE_TILES - 1), 0)

    def score(qi, par):
        j0 = first_tile(qi)
        keys = k_ref[pl.ds(j0, DENSE_TILES)].reshape(DENSE_TILES * ATT_TILE, PAIR_DIM)
        bias = bias_ref[qi - j0]
        qt = qt_ref[qi]
        for hd in range(HEADS_PER_PAIR):
            s = jnp.dot(keys, _head_rows(qt, hd), preferred_element_type=_f32) + bias
            s_ref[par, hd] = s
            m_ref[par, hd] = jnp.max(s, axis=0, keepdims=True)

    def value(qi, par):
        j0 = first_tile(qi)
        outs, lses = [], []
        for hd in range(HEADS_PER_PAIR):
            m = m_ref[par, hd]
            p = jnp.exp2(s_ref[par, hd] - m)
            vth = jnp.concatenate([vt_ref[j0 + t, hd * HEAD_DIM:(hd + 1) * HEAD_DIM, :]
                                   for t in range(DENSE_TILES)], axis=1)
            acc = jnp.dot(jnp.concatenate([vth, ones], axis=0), p.astype(_bf16), preferred_element_type=_f32)
            l = acc[HEAD_DIM:HEAD_DIM + 1, :]
            outs.append(acc[:HEAD_DIM, :] / l)
            lses.append(jnp.broadcast_to(m + jnp.log2(l), (HEAD_DIM, ATT_TILE)))
        o12 = jnp.concatenate(outs, axis=0).T
        e12 = jnp.concatenate(lses, axis=0).T
        o3 = o3_ref[qi].astype(_f32)
        e3 = lse3_ref[qi]
        top = jnp.maximum(e12, e3)
        w12 = jnp.exp2(e12 - top)
        w3 = jnp.exp2(e3 - top)
        row0 = qi * ATT_TILE
        if not isinstance(row0, int):
            row0 = pl.multiple_of(row0, ATT_TILE)
        o_ref[pl.ds(row0, ATT_TILE), :] = ((w12 * o12 + w3 * o3) / (w12 + w3)).astype(o_ref.dtype)

    nothing = lambda *a: None
    _pair_pipeline(nq, 0, score, value, nothing, nothing, nothing)


def _dilated_attention(qt, k, vt, bias, o3, lse3):
    b, n_pairs, nq = qt.shape[:3]
    assert nq >= DENSE_TILES
    tile_spec = lambda: pl.BlockSpec((None, None, nq, ATT_TILE, PAIR_DIM), lambda bi, pi: (bi, pi, 0, 0, 0))
    bias_spec = pl.BlockSpec(bias.shape, lambda bi, pi: (0, 0, 0), pipeline_mode=pl.Buffered(1))
    scratch = [pltpu.VMEM((2, HEADS_PER_PAIR, DENSE_TILES * ATT_TILE, ATT_TILE), _f32),
               pltpu.VMEM((2, HEADS_PER_PAIR, 1, ATT_TILE), _f32)]
    return _attention_call(_dilated_kernel, "dilated", qt, k, vt, [bias, o3, lse3],
                           [bias_spec, tile_spec(), tile_spec()], scratch)


def _post_kernel(x_ref, oa_ref, od_ref, ga_ref, gd_ref, wout_ref, gm_ref, wup_ref, wdown_ref,
                 gf_ref, o_ref, *, final_norm):
    da = oa_ref.shape[1]
    na = _rms_norm_rows(oa_ref[...].astype(_f32), ga_ref[...]).astype(_bf16)
    nd = _rms_norm_rows(od_ref[...].astype(_f32), gd_ref[...]).astype(_bf16)
    y = (x_ref[...]
         + jnp.dot(na, wout_ref[:da, :], preferred_element_type=_f32)
         + jnp.dot(nd, wout_ref[da:, :], preferred_element_type=_f32))
    h = _rms_norm_rows(y, gm_ref[...]).astype(_bf16)
    u = jnp.dot(h, wup_ref[...], preferred_element_type=_f32)
    u = jnp.square(jnp.maximum(u, 0.0)).astype(_bf16)
    y = y + jnp.dot(u, wdown_ref[...], preferred_element_type=_f32)
    if final_norm:
        y = _rms_norm_rows(y, gf_ref[...])
    o_ref[...] = y


def _post(x, oa, od, ga, gd, wout, gm, wup, wdown, gf, *, tm, final_norm):
    b, s, d = x.shape
    da = oa.shape[-1]
    dd = od.shape[-1]
    row = lambda bi, si: (bi, si, 0)
    const = lambda bi, si: (0, 0)
    whole = lambda a: pl.BlockSpec(a.shape, const, pipeline_mode=pl.Buffered(1))
    return pl.pallas_call(
        functools.partial(_post_kernel, final_norm=final_norm),
        grid=(b, s // tm),
        in_specs=[
            pl.BlockSpec((None, tm, d), row),
            pl.BlockSpec((None, tm, da), row),
            pl.BlockSpec((None, tm, dd), row),
            pl.BlockSpec((1, da), const),
            pl.BlockSpec((1, dd), const),
            whole(wout),
            pl.BlockSpec((1, d), const),
            whole(wup),
            whole(wdown),
            pl.BlockSpec((1, d), const),
        ],
        out_specs=pl.BlockSpec((None, tm, d), row),
        out_shape=jax.ShapeDtypeStruct((b, s, d), _f32),
        compiler_params=pltpu.CompilerParams(
            dimension_semantics=("parallel", "parallel"), vmem_limit_bytes=VMEM_LIMIT_BYTES),
        name="post",
    )(x, oa, od, ga, gd, wout, gm, wup, wdown, gf)


def _rope_tables(seq):
    inv = 1.0 / (ROPE_THETA ** (jnp.arange(0, HEAD_DIM, 2, dtype=_f32) / HEAD_DIM))
    ang = jnp.arange(seq, dtype=_f32)[:, None] * inv[None, :]
    cos, sin = jnp.cos(ang), jnp.sin(ang)
    reps = PAIR_DIM // (HEAD_DIM // 2)
    cos_t = jnp.tile(cos, (1, reps))
    sin_t = jnp.tile(jnp.concatenate([-sin, sin], axis=-1), (1, HEADS_PER_PAIR))
    return cos_t, sin_t


def kernel(x, attn_norm, w_in, moba_out_norm, dil_out_norm, w_out, mlp_norm, w_up, w_down, final_norm):
    depth = w_in.shape[0]
    seq = x.shape[1]
    cos_t, sin_t = _rope_tables(seq)
    bias = _dense_bias_table()
    band = _class_band_table()
    row = lambda v: v.reshape(1, -1)
    for l in range(depth):
        qta, ka, vta, qtd, kd, vtd, qrd, vrd = _inproj(
            x, row(attn_norm[l]), w_in[l].astype(_bf16), cos_t, sin_t, tm=512)
        out_a = _moba_attention(qta, ka, vta)
        o3, lse3 = _dil_class_attention(qrd, kd, vrd, band)
        out_d = _dilated_attention(qtd, kd, vtd, bias, o3, lse3)
        x = _post(x, out_a, out_d, row(moba_out_norm[l]), row(dil_out_norm[l]),
                  w_out[l].astype(_bf16), row(mlp_norm[l]), w_up[l].astype(_bf16),
                  w_down[l].astype(_bf16), row(final_norm), tm=512, final_norm=(l == depth - 1))
    return x
```

```python
import functools
import math

import jax
import jax.numpy as jnp
from jax import lax
from jax.experimental import pallas as pl
from jax.experimental.pallas import tpu as pltpu

HEAD_DIM = 64
HEADS_PER_PAIR = 2
PAIR_DIM = HEADS_PER_PAIR * HEAD_DIM
MOBA_BLOCK = 256
MOBA_TOPK = 3
DILATED_CONFIGS = ((128, 1), (512, 4), (2048, 16))
ROPE_THETA = 10000.0
NORM_EPS = 1e-6
NEG_INF = -1e30
ATT_TILE = 256
ONES_ROWS = 16
VMEM_LIMIT_BYTES = 56 * 1024 * 1024

_f32 = jnp.float32
_bf16 = jnp.bfloat16


def _rms_norm_rows(x, g):
    return x * lax.rsqrt(jnp.mean(x * x, axis=-1, keepdims=True) + NORM_EPS) * g


def _inproj_kernel(x_ref, g_ref, w_ref, cos_ref, sin_ref,
                   qta_ref, ka_ref, vta_ref, qtd_ref, kd_ref, vtd_ref, qrd_ref, vrd_ref, *, n_pairs, scale):
    tm = x_ref.shape[0]
    nt = tm // ATT_TILE
    h = _rms_norm_rows(x_ref[...], g_ref[...]).astype(_bf16)
    proj = jnp.dot(h, w_ref[...], preferred_element_type=_f32)
    cos = cos_ref[...]
    sin = sin_ref[...]
    lane = lax.broadcasted_iota(jnp.int32, (tm, PAIR_DIM), 1)
    first_half = (lane % HEAD_DIM) < (HEAD_DIM // 2)

    def rope(c):
        partner = jnp.where(first_half,
                            pltpu.roll(c, PAIR_DIM - HEAD_DIM // 2, 1),
                            pltpu.roll(c, HEAD_DIM // 2, 1))
        return c * cos + partner * sin

    def chunk(group, p):
        c0 = (group * n_pairs + p) * PAIR_DIM
        return proj[:, c0:c0 + PAIR_DIM]

    def put_transposed(ref, p, c):
        for t in range(nt):
            ref[p, t] = c[t * ATT_TILE:(t + 1) * ATT_TILE, :].T.astype(_bf16)

    def put_rows(ref, p, c):
        for t in range(nt):
            ref[p, t] = c[t * ATT_TILE:(t + 1) * ATT_TILE, :].astype(_bf16)

    for p in range(n_pairs):
        put_transposed(qta_ref, p, rope(chunk(0, p)) * scale)
        put_rows(ka_ref, p, rope(chunk(1, p)))
        put_transposed(vta_ref, p, chunk(2, p))
        qd = rope(chunk(3, p)) * scale
        put_transposed(qtd_ref, p, qd)
        put_rows(qrd_ref, p, qd)
        put_rows(kd_ref, p, rope(chunk(4, p)))
        put_transposed(vtd_ref, p, chunk(5, p))
        put_rows(vrd_ref, p, chunk(5, p))


def _inproj(x, g, w, cos_t, sin_t, *, tm):
    b, s, d = x.shape
    n_pairs = w.shape[1] // (6 * PAIR_DIM)
    nq = s // ATT_TILE
    nt = tm // ATT_TILE
    t_shape = jax.ShapeDtypeStruct((b, n_pairs, nq, PAIR_DIM, ATT_TILE), _bf16)
    r_shape = jax.ShapeDtypeStruct((b, n_pairs, nq, ATT_TILE, PAIR_DIM), _bf16)
    t_spec = pl.BlockSpec((None, n_pairs, nt, PAIR_DIM, ATT_TILE), lambda bi, si: (bi, 0, si, 0, 0))
    r_spec = pl.BlockSpec((None, n_pairs, nt, ATT_TILE, PAIR_DIM), lambda bi, si: (bi, 0, si, 0, 0))
    return pl.pallas_call(
        functools.partial(_inproj_kernel, n_pairs=n_pairs, scale=HEAD_DIM ** -0.5 * math.log2(math.e)),
        grid=(b, s // tm),
        in_specs=[
            pl.BlockSpec((None, tm, d), lambda bi, si: (bi, si, 0)),
            pl.BlockSpec((1, d), lambda bi, si: (0, 0)),
            pl.BlockSpec(w.shape, lambda bi, si: (0, 0), pipeline_mode=pl.Buffered(1)),
            pl.BlockSpec((tm, PAIR_DIM), lambda bi, si: (si, 0)),
            pl.BlockSpec((tm, PAIR_DIM), lambda bi, si: (si, 0)),
        ],
        out_specs=[t_spec, r_spec, t_spec, t_spec, r_spec, t_spec, r_spec, r_spec],
        out_shape=[t_shape, r_shape, t_shape, t_shape, r_shape, t_shape, r_shape, r_shape],
        compiler_params=pltpu.CompilerParams(
            dimension_semantics=("parallel", "parallel"), vmem_limit_bytes=VMEM_LIMIT_BYTES),
        name="inproj",
    )(x, g, w, cos_t, sin_t)


def _head_rows(qt, head):
    row = lax.broadcasted_iota(jnp.int32, qt.shape, 0)
    mine = (row >= head * HEAD_DIM) & (row < (head + 1) * HEAD_DIM)
    return jnp.where(mine, qt, jnp.zeros_like(qt))


def _write_out(o_ref, qi, outs):
    o_t = jnp.concatenate(outs, axis=0)
    row0 = qi * ATT_TILE
    if not isinstance(row0, int):
        row0 = pl.multiple_of(row0, ATT_TILE)
    o_ref[pl.ds(row0, ATT_TILE), :] = o_t.T.astype(o_ref.dtype)


def _pair_pipeline(n_pairs, n_steps, prepare, value_first, score_step, value_step, finish):
    def stage(score_pair, value_pair, score_par):
        value_par = 1 - score_par
        if score_pair is not None:
            prepare(score_pair, score_par)
        if value_pair is not None:
            value_first(value_pair, value_par)
        for step in range(n_steps):
            if value_pair is not None:
                value_step(value_pair, value_par, step)
            if score_pair is not None:
                score_step(score_pair, score_par, step)
        if value_pair is not None:
            finish(value_pair)

    assert n_pairs >= 2 and n_pairs % 2 == 0
    stage(0, None, 0)

    def two_pairs(u, carry):
        stage(2 * u + 1, 2 * u, 1)
        stage(2 * u + 2, 2 * u + 1, 0)
        return carry

    lax.fori_loop(0, n_pairs // 2 - 1, two_pairs, 0)
    stage(n_pairs - 1, n_pairs - 2, 1)
    stage(None, n_pairs - 1, 0)


def _attention_call(kernel, name, qt, k, vt, extra_inputs, extra_specs, scratch):
    b, n_pairs, nq = qt.shape[:3]
    s = nq * ATT_TILE
    t_spec = pl.BlockSpec((None, None, nq, PAIR_DIM, ATT_TILE), lambda bi, pi: (bi, pi, 0, 0, 0))
    r_spec = pl.BlockSpec((None, None, nq, ATT_TILE, PAIR_DIM), lambda bi, pi: (bi, pi, 0, 0, 0))
    return pl.pallas_call(
        kernel,
        grid=(b, n_pairs),
        in_specs=[t_spec, r_spec, t_spec] + extra_specs,
        out_specs=pl.BlockSpec((None, s, PAIR_DIM), lambda bi, pi: (bi, 0, pi)),
        out_shape=jax.ShapeDtypeStruct((b, s, n_pairs * PAIR_DIM), _bf16),
        scratch_shapes=scratch,
        compiler_params=pltpu.CompilerParams(
            dimension_semantics=("parallel", "parallel"), vmem_limit_bytes=VMEM_LIMIT_BYTES),
        name=name,
    )(qt, k, vt, *extra_inputs)


def _moba_kernel(qt_ref, k_ref, vt_ref, aux_ref, o_ref,
                 kmean_ref, w_ref, sd_ref, s_ref, m_ref, acc_ref):
    nq = qt_ref.shape[0]
    n_steps = nq // 2
    chunk = 2 * ATT_TILE

    for j in range(nq):
        kmean_ref[j:j + 1, :] = jnp.mean(k_ref[j].astype(_f32), axis=0, keepdims=True)
    km = kmean_ref[...]
    km_hi = km.astype(_bf16)
    km_lo = (km - km_hi.astype(_f32)).astype(_bf16)
    for par in range(2):
        for slot in range(2):
            for hd in range(HEADS_PER_PAIR):
                w_ref[par, slot, hd, PAIR_DIM + nq:, :] = jnp.zeros((PAIR_DIM - nq, ATT_TILE), _bf16)

    key_i = lax.broadcasted_iota(jnp.int32, (ATT_TILE, ATT_TILE), 0)
    qry_i = lax.broadcasted_iota(jnp.int32, (ATT_TILE, ATT_TILE), 1)
    causal = key_i <= qry_i
    blk = lax.broadcasted_iota(jnp.int32, (nq, ATT_TILE), 0)
    ones = jnp.ones((ONES_ROWS, ATT_TILE), _bf16)

    def head_vt(j, hd):
        return jnp.concatenate([vt_ref[j, hd * HEAD_DIM:(hd + 1) * HEAD_DIM, :], ones], axis=0)

    def tile_of(pair, slot):
        return pair if slot == 0 else nq - 1 - pair

    def plan(pair, step):
        n_first = (pair + 1) // 2
        first = step < n_first
        return jnp.where(first, 0, 1), jnp.where(first, step, step - n_first)

    def prepare(pair, par):
        for slot in range(2):
            qi = tile_of(pair, slot)
            qt = qt_ref[qi]
            past = blk < qi
            for hd in range(HEADS_PER_PAIR):
                qth = _head_rows(qt, hd)
                gate = (jnp.dot(km_hi, qth, preferred_element_type=_f32)
                        + jnp.dot(km_lo, qth, preferred_element_type=_f32))
                gate = jnp.where(past, gate, NEG_INF)
                rank = jnp.zeros(gate.shape, _f32)
                for i in range(nq):
                    gi = gate[i:i + 1, :]
                    rank = rank + jnp.where(blk > i, jnp.where(gi >= gate, 1.0, 0.0),
                                            jnp.where(gi > gate, 1.0, 0.0))
                sel = jnp.where(past, jnp.where(rank < MOBA_TOPK, 0.0, NEG_INF), NEG_INF)
                w_ref[par, slot, hd, :PAIR_DIM, :] = qth
                w_ref[par, slot, hd, PAIR_DIM:PAIR_DIM + nq, :] = sel.astype(_bf16)
                s = jnp.dot(k_ref[qi], qth, preferred_element_type=_f32)
                s = jnp.where(causal, s, NEG_INF)
                sd_ref[par, slot, hd] = s
                m_ref[par, slot, hd] = jnp.max(s, axis=0, keepdims=True)

    def score_step(pair, par, step):
        slot, c = plan(pair, step)
        keys = k_ref[pl.ds(2 * c, 2)].reshape(chunk, PAIR_DIM)
        lhs = jnp.concatenate([keys, aux_ref[c]], axis=1)
        for hd in range(HEADS_PER_PAIR):
            s = jnp.dot(lhs, w_ref[par, slot, hd], preferred_element_type=_f32)
            s_ref[par, step, hd] = s
            m_ref[par, slot, hd] = jnp.maximum(m_ref[par, slot, hd], jnp.max(s, axis=0, keepdims=True))

    def value_first(pair, par):
        for slot in range(2):
            qi = tile_of(pair, slot)
            for hd in range(HEADS_PER_PAIR):
                p = jnp.exp2(sd_ref[par, slot, hd] - m_ref[par, slot, hd])
                acc_ref[slot, hd] = jnp.dot(head_vt(qi, hd), p.astype(_bf16), preferred_element_type=_f32)

    def value_step(pair, par, step):
        slot, c = plan(pair, step)
        for hd in range(HEADS_PER_PAIR):
            p = jnp.exp2(s_ref[par, step, hd] - m_ref[par, slot, hd])
            vtc = jnp.concatenate([head_vt(2 * c, hd), head_vt(2 * c + 1, hd)], axis=1)
            acc_ref[slot, hd] += jnp.dot(vtc, p.astype(_bf16), preferred_element_type=_f32)

    def finish(pair):
        for slot in range(2):
            _write_out(o_ref, tile_of(pair, slot),
                       [acc_ref[slot, hd, :HEAD_DIM, :] / acc_ref[slot, hd, HEAD_DIM:HEAD_DIM + 1, :]
                        for hd in range(HEADS_PER_PAIR)])

    _pair_pipeline(nq // 2, n_steps, prepare, value_first, score_step, value_step, finish)


def _moba_block_onehot(nq):
    c = jnp.arange(nq // 2, dtype=jnp.int32)[:, None, None]
    r = jnp.arange(2 * ATT_TILE, dtype=jnp.int32)[None, :, None]
    lane = jnp.arange(PAIR_DIM, dtype=jnp.int32)[None, None, :]
    return (lane == 2 * c + r // ATT_TILE).astype(_bf16)


def _moba_attention(qt, k, vt):
    nq = qt.shape[2]
    aux = _moba_block_onehot(nq)
    aux_spec = pl.BlockSpec(aux.shape, lambda bi, pi: (0, 0, 0), pipeline_mode=pl.Buffered(1))
    scratch = [pltpu.VMEM((nq, PAIR_DIM), _f32),
               pltpu.VMEM((2, 2, HEADS_PER_PAIR, 2 * PAIR_DIM, ATT_TILE), _bf16),
               pltpu.VMEM((2, 2, HEADS_PER_PAIR, ATT_TILE, ATT_TILE), _f32),
               pltpu.VMEM((2, nq // 2, HEADS_PER_PAIR, 2 * ATT_TILE, ATT_TILE), _f32),
               pltpu.VMEM((2, 2, HEADS_PER_PAIR, 1, ATT_TILE), _f32),
               pltpu.VMEM((2, HEADS_PER_PAIR, HEAD_DIM + ONES_ROWS, ATT_TILE), _f32)]
    return _attention_call(_moba_kernel, "moba", qt, k, vt, [aux], [aux_spec], scratch)


DENSE_CONFIGS = DILATED_CONFIGS[:2]
SPARSE_WINDOW, SPARSE_DILATION = DILATED_CONFIGS[2]
DENSE_TILES = max(w for w, _ in DENSE_CONFIGS) // ATT_TILE + 1
CLASSES_PER_ITEM = 2


def _dense_bias_table():
    v = jnp.arange(DENSE_TILES, dtype=jnp.int32)[:, None, None]
    row = jnp.arange(DENSE_TILES * ATT_TILE, dtype=jnp.int32)[None, :, None]
    qry = jnp.arange(ATT_TILE, dtype=jnp.int32)[None, None, :]
    delta = v * ATT_TILE + qry - row
    count = jnp.zeros(delta.shape, _f32)
    for window, dilation in DENSE_CONFIGS:
        hit = (delta >= 0) & (delta <= window) & (delta % dilation == 0)
        count = count + hit.astype(_f32)
    return jnp.where(count > 0, jnp.log2(jnp.maximum(count, 1.0)), NEG_INF)


def _class_band_table():
    key = jnp.arange(ATT_TILE, dtype=jnp.int32)[:, None]
    qry = jnp.arange(ATT_TILE, dtype=jnp.int32)[None, :]
    delta = qry - key
    return jnp.where((delta >= 0) & (delta <= SPARSE_WINDOW // SPARSE_DILATION), 0.0, NEG_INF).astype(_f32)


def _dilated_kernel(qt_ref, k_ref, vt_ref, bias_ref, band_ref, qr_ref, vr_ref, o_ref,
                    s_ref, m_ref, xq_ref, xk_ref, xv_ref, o3_ref, e3_ref, s3_ref, m3_ref, vt3_ref):
    nq = qt_ref.shape[0]

    for t in range(nq):
        rows = slice(t * ATT_TILE, (t + 1) * ATT_TILE)
        xq_ref[rows, :] = qr_ref[t].astype(_f32)
        xk_ref[rows, :] = k_ref[t].astype(_f32)
        xv_ref[rows, :] = vr_ref[t].astype(_f32)
    class_ones = jnp.ones((ONES_ROWS, ATT_TILE), _bf16)

    def members(item, c):
        return pl.ds(CLASSES_PER_ITEM * item + c, ATT_TILE, stride=SPARSE_DILATION)

    def class_score(item, par):
        band = band_ref[...]
        for c in range(CLASSES_PER_ITEM):
            qt = xq_ref[members(item, c), :].T.astype(_bf16)
            keys = xk_ref[members(item, c), :].astype(_bf16)
            vt3_ref[par, c] = xv_ref[members(item, c), :].T.astype(_bf16)
            for hd in range(HEADS_PER_PAIR):
                s = jnp.dot(keys, _head_rows(qt, hd), preferred_element_type=_f32) + band
                s3_ref[par, c, hd] = s
                m3_ref[par, c, hd] = jnp.max(s, axis=0, keepdims=True)

    def class_value(item, par):
        for c in range(CLASSES_PER_ITEM):
            outs, lses = [], []
            for hd in range(HEADS_PER_PAIR):
                m = m3_ref[par, c, hd]
                p = jnp.exp2(s3_ref[par, c, hd] - m)
                vth = jnp.concatenate([vt3_ref[par, c, hd * HEAD_DIM:(hd + 1) * HEAD_DIM, :], class_ones], axis=0)
                acc = jnp.dot(vth, p.astype(_bf16), preferred_element_type=_f32)
                l = acc[HEAD_DIM:HEAD_DIM + 1, :]
                outs.append(acc[:HEAD_DIM, :] / l)
                lses.append(jnp.broadcast_to(m + jnp.log2(l), (HEAD_DIM, ATT_TILE)))
            o3_ref[members(item, c), :] = jnp.concatenate(outs, axis=0).T
            e3_ref[members(item, c), :] = jnp.concatenate(lses, axis=0).T

    nothing = lambda *a: None
    _pair_pipeline(SPARSE_DILATION // CLASSES_PER_ITEM, 0, class_score, class_value, nothing, nothing, nothing)

    ones = jnp.ones((ONES_ROWS, DENSE_TILES * ATT_TILE), _bf16)

    def first_tile(qi):
        return jnp.maximum(qi - (DENSE_TILES - 1), 0)

    def score(qi, par):
        j0 = first_tile(qi)
        keys = k_ref[pl.ds(j0, DENSE_TILES)].reshape(DENSE_TILES * ATT_TILE, PAIR_DIM)
        bias = bias_ref[qi - j0]
        qt = qt_ref[qi]
        for hd in range(HEADS_PER_PAIR):
            s = jnp.dot(keys, _head_rows(qt, hd), preferred_element_type=_f32) + bias
            s_ref[par, hd] = s
            m_ref[par, hd] = jnp.max(s, axis=0, keepdims=True)

    def value(qi, par):
        j0 = first_tile(qi)
        outs, lses = [], []
        for hd in range(HEADS_PER_PAIR):
            m = m_ref[par, hd]
            p = jnp.exp2(s_ref[par, hd] - m)
            vth = jnp.concatenate([vt_ref[j0 + t, hd * HEAD_DIM:(hd + 1) * HEAD_DIM, :]
                                   for t in range(DENSE_TILES)], axis=1)
            acc = jnp.dot(jnp.concatenate([vth, ones], axis=0), p.astype(_bf16), preferred_element_type=_f32)
            l = acc[HEAD_DIM:HEAD_DIM + 1, :]
            outs.append(acc[:HEAD_DIM, :] / l)
            lses.append(jnp.broadcast_to(m + jnp.log2(l), (HEAD_DIM, ATT_TILE)))
        o12 = jnp.concatenate(outs, axis=0).T
        e12 = jnp.concatenate(lses, axis=0).T
        row0 = qi * ATT_TILE
        if not isinstance(row0, int):
            row0 = pl.multiple_of(row0, ATT_TILE)
        rows = pl.ds(row0, ATT_TILE)
        o3 = o3_ref[rows, :]
        e3 = e3_ref[rows, :]
        top = jnp.maximum(e12, e3)
        w12 = jnp.exp2(e12 - top)
        w3 = jnp.exp2(e3 - top)
        o_ref[rows, :] = ((w12 * o12 + w3 * o3) / (w12 + w3)).astype(o_ref.dtype)

    _pair_pipeline(nq, 0, score, value, nothing, nothing, nothing)


def _dilated_attention(qt, k, vt, q_rows, v_rows, bias, band):
    b, n_pairs, nq = qt.shape[:3]
    s = nq * ATT_TILE
    assert nq >= DENSE_TILES and s % SPARSE_DILATION == 0 and s // SPARSE_DILATION == ATT_TILE
    tile_spec = lambda: pl.BlockSpec((None, None, nq, ATT_TILE, PAIR_DIM), lambda bi, pi: (bi, pi, 0, 0, 0))
    bias_spec = pl.BlockSpec(bias.shape, lambda bi, pi: (0, 0, 0), pipeline_mode=pl.Buffered(1))
    band_spec = pl.BlockSpec(band.shape, lambda bi, pi: (0, 0), pipeline_mode=pl.Buffered(1))
    token_major = lambda: pltpu.VMEM((s, PAIR_DIM), _f32)
    scratch = [pltpu.VMEM((2, HEADS_PER_PAIR, DENSE_TILES * ATT_TILE, ATT_TILE), _f32),
               pltpu.VMEM((2, HEADS_PER_PAIR, 1, ATT_TILE), _f32),
               token_major(), token_major(), token_major(),
               token_major(), token_major(),
               pltpu.VMEM((2, CLASSES_PER_ITEM, HEADS_PER_PAIR, ATT_TILE, ATT_TILE), _f32),
               pltpu.VMEM((2, CLASSES_PER_ITEM, HEADS_PER_PAIR, 1, ATT_TILE), _f32),
               pltpu.VMEM((2, CLASSES_PER_ITEM, PAIR_DIM, ATT_TILE), _bf16)]
    return _attention_call(_dilated_kernel, "dilated", qt, k, vt, [bias, band, q_rows, v_rows],
                           [bias_spec, band_spec, tile_spec(), tile_spec()], scratch)


def _post_kernel(x_ref, oa_ref, od_ref, ga_ref, gd_ref, wout_ref, gm_ref, wup_ref, wdown_ref,
                 gf_ref, o_ref, *, final_norm):
    da = oa_ref.shape[1]
    na = _rms_norm_rows(oa_ref[...].astype(_f32), ga_ref[...]).astype(_bf16)
    nd = _rms_norm_rows(od_ref[...].astype(_f32), gd_ref[...]).astype(_bf16)
    y = (x_ref[...]
         + jnp.dot(na, wout_ref[:da, :], preferred_element_type=_f32)
         + jnp.dot(nd, wout_ref[da:, :], preferred_element_type=_f32))
    h = _rms_norm_rows(y, gm_ref[...]).astype(_bf16)
    u = jnp.dot(h, wup_ref[...], preferred_element_type=_f32)
    u = jnp.square(jnp.maximum(u, 0.0)).astype(_bf16)
    y = y + jnp.dot(u, wdown_ref[...], preferred_element_type=_f32)
    if final_norm:
        y = _rms_norm_rows(y, gf_ref[...])
    o_ref[...] = y


def _post(x, oa, od, ga, gd, wout, gm, wup, wdown, gf, *, tm, final_norm):
    b, s, d = x.shape
    da = oa.shape[-1]
    dd = od.shape[-1]
    row = lambda bi, si: (bi, si, 0)
    const = lambda bi, si: (0, 0)
    whole = lambda a: pl.BlockSpec(a.shape, const, pipeline_mode=pl.Buffered(1))
    return pl.pallas_call(
        functools.partial(_post_kernel, final_norm=final_norm),
        grid=(b, s // tm),
        in_specs=[
            pl.BlockSpec((None, tm, d), row),
            pl.BlockSpec((None, tm, da), row),
            pl.BlockSpec((None, tm, dd), row),
            pl.BlockSpec((1, da), const),
            pl.BlockSpec((1, dd), const),
            whole(wout),
            pl.BlockSpec((1, d), const),
            whole(wup),
            whole(wdown),
            pl.BlockSpec((1, d), const),
        ],
        out_specs=pl.BlockSpec((None, tm, d), row),
        out_shape=jax.ShapeDtypeStruct((b, s, d), _f32),
        compiler_params=pltpu.CompilerParams(
            dimension_semantics=("parallel", "parallel"), vmem_limit_bytes=VMEM_LIMIT_BYTES),
        name="post",
    )(x, oa, od, ga, gd, wout, gm, wup, wdown, gf)


def _rope_tables(seq):
    inv = 1.0 / (ROPE_THETA ** (jnp.arange(0, HEAD_DIM, 2, dtype=_f32) / HEAD_DIM))
    ang = jnp.arange(seq, dtype=_f32)[:, None] * inv[None, :]
    cos, sin = jnp.cos(ang), jnp.sin(ang)
    reps = PAIR_DIM // (HEAD_DIM // 2)
    cos_t = jnp.tile(cos, (1, reps))
    sin_t = jnp.tile(jnp.concatenate([-sin, sin], axis=-1), (1, HEADS_PER_PAIR))
    return cos_t, sin_t


def kernel(x, attn_norm, w_in, moba_out_norm, dil_out_norm, w_out, mlp_norm, w_up, w_down, final_norm):
    depth = w_in.shape[0]
    seq = x.shape[1]
    cos_t, sin_t = _rope_tables(seq)
    bias = _dense_bias_table()
    band = _class_band_table()
    row = lambda v: v.reshape(1, -1)
    for l in range(depth):
        qta, ka, vta, qtd, kd, vtd, qrd, vrd = _inproj(
            x, row(attn_norm[l]), w_in[l].astype(_bf16), cos_t, sin_t, tm=512)
        out_a = _moba_attention(qta, ka, vta)
        out_d = _dilated_attention(qtd, kd, vtd, qrd, vrd, bias, band)
        x = _post(x, out_a, out_d, row(moba_out_norm[l]), row(dil_out_norm[l]),
                  w_out[l].astype(_bf16), row(mlp_norm[l]), w_up[l].astype(_bf16),
                  w_down[l].astype(_bf16), row(final_norm), tm=512, final_norm=(l == depth - 1))
    return x
```

```python
import functools
import math

import jax
import jax.numpy as jnp
from jax import lax
from jax.experimental import pallas as pl
from jax.experimental.pallas import tpu as pltpu

HEAD_DIM = 64
HEADS_PER_PAIR = 2
PAIR_DIM = HEADS_PER_PAIR * HEAD_DIM
MOBA_BLOCK = 256
MOBA_TOPK = 3
DILATED_CONFIGS = ((128, 1), (512, 4), (2048, 16))
ROPE_THETA = 10000.0
NORM_EPS = 1e-6
NEG_INF = -1e30
ATT_TILE = 256
ONES_ROWS = 16
VMEM_LIMIT_BYTES = 56 * 1024 * 1024

_f32 = jnp.float32
_bf16 = jnp.bfloat16


def _rms_norm_rows(x, g):
    return x * lax.rsqrt(jnp.mean(x * x, axis=-1, keepdims=True) + NORM_EPS) * g


def _inproj_kernel(x_ref, g_ref, w_ref, cos_ref, sin_ref,
                   qta_ref, ka_ref, vta_ref, qtd_ref, kd_ref, vtd_ref, *, n_pairs, scale):
    tm = x_ref.shape[0]
    nt = tm // ATT_TILE
    h = _rms_norm_rows(x_ref[...], g_ref[...]).astype(_bf16)
    proj = jnp.dot(h, w_ref[...], preferred_element_type=_f32)
    cos = cos_ref[...]
    sin = sin_ref[...]
    lane = lax.broadcasted_iota(jnp.int32, (tm, PAIR_DIM), 1)
    first_half = (lane % HEAD_DIM) < (HEAD_DIM // 2)

    def rope(c):
        partner = jnp.where(first_half,
                            pltpu.roll(c, PAIR_DIM - HEAD_DIM // 2, 1),
                            pltpu.roll(c, HEAD_DIM // 2, 1))
        return c * cos + partner * sin

    def chunk(group, p):
        c0 = (group * n_pairs + p) * PAIR_DIM
        return proj[:, c0:c0 + PAIR_DIM]

    def put_transposed(ref, p, c):
        for t in range(nt):
            ref[p, t] = c[t * ATT_TILE:(t + 1) * ATT_TILE, :].T.astype(_bf16)

    def put_rows(ref, p, c):
        for t in range(nt):
            ref[p, t] = c[t * ATT_TILE:(t + 1) * ATT_TILE, :].astype(_bf16)

    for p in range(n_pairs):
        put_transposed(qta_ref, p, rope(chunk(0, p)) * scale)
        put_rows(ka_ref, p, rope(chunk(1, p)))
        put_transposed(vta_ref, p, chunk(2, p))
        put_transposed(qtd_ref, p, rope(chunk(3, p)) * scale)
        put_rows(kd_ref, p, rope(chunk(4, p)))
        put_transposed(vtd_ref, p, chunk(5, p))


def _inproj(x, g, w_all, layer, cos_t, sin_t, *, tm):
    b, s, d = x.shape
    n_pairs = w_all.shape[2] // (6 * PAIR_DIM)
    nq = s // ATT_TILE
    nt = tm // ATT_TILE
    t_shape = jax.ShapeDtypeStruct((b, n_pairs, nq, PAIR_DIM, ATT_TILE), _bf16)
    r_shape = jax.ShapeDtypeStruct((b, n_pairs, nq, ATT_TILE, PAIR_DIM), _bf16)
    t_spec = pl.BlockSpec((None, n_pairs, nt, PAIR_DIM, ATT_TILE), lambda bi, si: (bi, 0, si, 0, 0))
    r_spec = pl.BlockSpec((None, n_pairs, nt, ATT_TILE, PAIR_DIM), lambda bi, si: (bi, 0, si, 0, 0))
    return pl.pallas_call(
        functools.partial(_inproj_kernel, n_pairs=n_pairs, scale=HEAD_DIM ** -0.5 * math.log2(math.e)),
        grid=(b, s // tm),
        in_specs=[
            pl.BlockSpec((None, tm, d), lambda bi, si: (bi, si, 0)),
            pl.BlockSpec((1, d), lambda bi, si: (0, 0)),
            pl.BlockSpec((None,) + w_all.shape[1:], lambda bi, si: (layer, 0, 0), pipeline_mode=pl.Buffered(1)),
            pl.BlockSpec((tm, PAIR_DIM), lambda bi, si: (si, 0)),
            pl.BlockSpec((tm, PAIR_DIM), lambda bi, si: (si, 0)),
        ],
        out_specs=[t_spec, r_spec, t_spec, t_spec, r_spec, t_spec],
        out_shape=[t_shape, r_shape, t_shape, t_shape, r_shape, t_shape],
        compiler_params=pltpu.CompilerParams(
            dimension_semantics=("parallel", "parallel"), vmem_limit_bytes=VMEM_LIMIT_BYTES),
        name="inproj",
    )(x, g, w_all, cos_t, sin_t)


def _head_rows(qt, head):
    row = lax.broadcasted_iota(jnp.int32, qt.shape, 0)
    mine = (row >= head * HEAD_DIM) & (row < (head + 1) * HEAD_DIM)
    return jnp.where(mine, qt, jnp.zeros_like(qt))


def _pair_pipeline(n_pairs, stage):
    assert n_pairs >= 2 and n_pairs % 2 == 0
    stage(0, None, 0)

    def two_pairs(u, carry):
        stage(2 * u + 1, 2 * u, 1)
        stage(2 * u + 2, 2 * u + 1, 0)
        return carry

    lax.fori_loop(0, n_pairs // 2 - 1, two_pairs, 0)
    stage(n_pairs - 1, n_pairs - 2, 1)
    stage(None, n_pairs - 1, 0)


DENSE_CONFIGS = DILATED_CONFIGS[:2]
SPARSE_WINDOW, SPARSE_DILATION = DILATED_CONFIGS[2]
DENSE_TILES = max(w for w, _ in DENSE_CONFIGS) // ATT_TILE + 1
CLASSES_PER_ITEM = 2
TILES_PER_ITEM = 2
DENSE_AT_STEP = (1, 5)
CLASS_AT_STEP = 3
OWN_BLOCK_AT_STEP = 6


def _moba_block_onehot(nq):
    c = jnp.arange(nq // 2, dtype=jnp.int32)[:, None, None]
    r = jnp.arange(2 * ATT_TILE, dtype=jnp.int32)[None, :, None]
    lane = jnp.arange(PAIR_DIM, dtype=jnp.int32)[None, None, :]
    return (lane == 2 * c + r // ATT_TILE).astype(_bf16)


def _dense_bias_table():
    v = jnp.arange(DENSE_TILES, dtype=jnp.int32)[:, None, None]
    row = jnp.arange(DENSE_TILES * ATT_TILE, dtype=jnp.int32)[None, :, None]
    qry = jnp.arange(ATT_TILE, dtype=jnp.int32)[None, None, :]
    delta = v * ATT_TILE + qry - row
    count = jnp.zeros(delta.shape, _f32)
    for window, dilation in DENSE_CONFIGS:
        hit = (delta >= 0) & (delta <= window) & (delta % dilation == 0)
        count = count + hit.astype(_f32)
    return jnp.where(count > 0, jnp.log2(jnp.maximum(count, 1.0)), NEG_INF)


def _class_band_table():
    key = jnp.arange(ATT_TILE, dtype=jnp.int32)[:, None]
    qry = jnp.arange(ATT_TILE, dtype=jnp.int32)[None, :]
    delta = qry - key
    return jnp.where((delta >= 0) & (delta <= SPARSE_WINDOW // SPARSE_DILATION), 0.0, NEG_INF).astype(_f32)


def _tile_rows(qi):
    row0 = qi * ATT_TILE
    if not isinstance(row0, int):
        row0 = pl.multiple_of(row0, ATT_TILE)
    return pl.ds(row0, ATT_TILE)


def _mixers_kernel(qta_ref, ka_ref, vta_ref, aux_ref, qtd_ref, kd_ref, vtd_ref, bias_ref, band_ref,
                   oa_ref, od_ref,
                   kmean_ref, w_ref, sd_ref, sa_ref, ma_ref, acc_ref,
                   s12_ref, m12_ref, xq_ref, xk_ref, xv_ref, o3_ref, e3_ref, s3_ref, m3_ref, vt3_ref,
                   o12_ref, e12_ref):
    nq = qta_ref.shape[0]
    n_steps = nq // 2
    chunk = 2 * ATT_TILE
    ones = jnp.ones((ONES_ROWS, ATT_TILE), _bf16)

    for j in range(nq):
        kmean_ref[j:j + 1, :] = jnp.mean(ka_ref[j].astype(_f32), axis=0, keepdims=True)
    km = kmean_ref[...]
    km_hi = km.astype(_bf16)
    km_lo = (km - km_hi.astype(_f32)).astype(_bf16)
    for slot in range(2):
        for hd in range(HEADS_PER_PAIR):
            w_ref[slot, hd, PAIR_DIM + nq:, :] = jnp.zeros((PAIR_DIM - nq, ATT_TILE), _bf16)

    key_i = lax.broadcasted_iota(jnp.int32, (ATT_TILE, ATT_TILE), 0)
    qry_i = lax.broadcasted_iota(jnp.int32, (ATT_TILE, ATT_TILE), 1)
    causal = key_i <= qry_i
    blk = lax.broadcasted_iota(jnp.int32, (nq, ATT_TILE), 0)

    def moba_vt(j, hd):
        return jnp.concatenate([vta_ref[j, hd * HEAD_DIM:(hd + 1) * HEAD_DIM, :], ones], axis=0)

    def tile_of(pair, slot):
        return pair if slot == 0 else nq - 1 - pair

    def plan(pair, step):
        n_first = (pair + 1) // 2
        first = step < n_first
        return jnp.where(first, 0, 1), jnp.where(first, step, step - n_first)

    def moba_prepare(pair, par):
        for slot in range(2):
            qi = tile_of(pair, slot)
            qt = qta_ref[qi]
            past = blk < qi
            for hd in range(HEADS_PER_PAIR):
                qth = _head_rows(qt, hd)
                gate = (jnp.dot(km_hi, qth, preferred_element_type=_f32)
                        + jnp.dot(km_lo, qth, preferred_element_type=_f32))
                gate = jnp.where(past, gate, NEG_INF)
                rank = jnp.zeros(gate.shape, _f32)
                for i in range(nq):
                    gi = gate[i:i + 1, :]
                    rank = rank + jnp.where(blk > i, jnp.where(gi >= gate, 1.0, 0.0),
                                            jnp.where(gi > gate, 1.0, 0.0))
                sel = jnp.where(past, jnp.where(rank < MOBA_TOPK, 0.0, NEG_INF), NEG_INF)
                w_ref[slot, hd, :PAIR_DIM, :] = qth
                w_ref[slot, hd, PAIR_DIM:PAIR_DIM + nq, :] = sel.astype(_bf16)
                s = jnp.dot(ka_ref[qi], qth, preferred_element_type=_f32)
                s = jnp.where(causal, s, NEG_INF)
                sd_ref[par, slot, hd] = s
                ma_ref[par, slot, hd] = jnp.max(s, axis=0, keepdims=True)

    def moba_score_step(pair, par, step):
        slot, c = plan(pair, step)
        keys = ka_ref[pl.ds(2 * c, 2)].reshape(chunk, PAIR_DIM)
        lhs = jnp.concatenate([keys, aux_ref[c]], axis=1)
        for hd in range(HEADS_PER_PAIR):
            s = jnp.dot(lhs, w_ref[slot, hd], preferred_element_type=_f32)
            sa_ref[step, hd] = s
            ma_ref[par, slot, hd] = jnp.maximum(ma_ref[par, slot, hd], jnp.max(s, axis=0, keepdims=True))

    def moba_value_first(pair, par):
        for slot in range(2):
            qi = tile_of(pair, slot)
            for hd in range(HEADS_PER_PAIR):
                p = jnp.exp2(sd_ref[par, slot, hd] - ma_ref[par, slot, hd])
                acc_ref[slot, hd] += jnp.dot(moba_vt(qi, hd), p.astype(_bf16), preferred_element_type=_f32)

    def moba_value_step(pair, par, step):
        slot, c = plan(pair, step)
        for hd in range(HEADS_PER_PAIR):
            p = jnp.exp2(sa_ref[step, hd] - ma_ref[par, slot, hd])
            vtc = jnp.concatenate([moba_vt(2 * c, hd), moba_vt(2 * c + 1, hd)], axis=1)
            acc_ref[slot, hd] += jnp.dot(vtc, p.astype(_bf16), preferred_element_type=_f32)

    def moba_finish(pair):
        for slot in range(2):
            o_t = jnp.concatenate([acc_ref[slot, hd, :HEAD_DIM, :] / acc_ref[slot, hd, HEAD_DIM:HEAD_DIM + 1, :]
                                   for hd in range(HEADS_PER_PAIR)], axis=0)
            oa_ref[_tile_rows(tile_of(pair, slot)), :] = o_t.T.astype(oa_ref.dtype)

    for t in range(nq):
        rows = slice(t * ATT_TILE, (t + 1) * ATT_TILE)
        xq_ref[rows, :] = qtd_ref[t].astype(_f32).T
        xk_ref[rows, :] = kd_ref[t].astype(_f32)
        xv_ref[rows, :] = vtd_ref[t].astype(_f32).T

    def members(item, c):
        return pl.ds(CLASSES_PER_ITEM * item + c, ATT_TILE, stride=SPARSE_DILATION)

    def normalise(acc, m):
        l = acc[HEAD_DIM:HEAD_DIM + 1, :]
        return acc[:HEAD_DIM, :] / l, jnp.broadcast_to(m + jnp.log2(l), (HEAD_DIM, ATT_TILE))

    def class_score(item, par):
        band = band_ref[...]
        for c in range(CLASSES_PER_ITEM):
            qt = xq_ref[members(item, c), :].T.astype(_bf16)
            keys = xk_ref[members(item, c), :].astype(_bf16)
            vt3_ref[par, c] = xv_ref[members(item, c), :].T.astype(_bf16)
            for hd in range(HEADS_PER_PAIR):
                s = jnp.dot(keys, _head_rows(qt, hd), preferred_element_type=_f32) + band
                s3_ref[par, c, hd] = s
                m3_ref[par, c, hd] = jnp.max(s, axis=0, keepdims=True)

    def class_value(item, par):
        for c in range(CLASSES_PER_ITEM):
            outs, lses = [], []
            for hd in range(HEADS_PER_PAIR):
                m = m3_ref[par, c, hd]
                p = jnp.exp2(s3_ref[par, c, hd] - m)
                vth = jnp.concatenate([vt3_ref[par, c, hd * HEAD_DIM:(hd + 1) * HEAD_DIM, :], ones], axis=0)
                o, e = normalise(jnp.dot(vth, p.astype(_bf16), preferred_element_type=_f32), m)
                outs.append(o)
                lses.append(e)
            o3_ref[members(item, c), :] = jnp.concatenate(outs, axis=0).T
            e3_ref[members(item, c), :] = jnp.concatenate(lses, axis=0).T

    dense_ones = jnp.ones((ONES_ROWS, DENSE_TILES * ATT_TILE), _bf16)

    def first_tile(qi):
        return jnp.maximum(qi - (DENSE_TILES - 1), 0)

    def dense_score(item, t):
        qi = TILES_PER_ITEM * item + t
        j0 = first_tile(qi)
        keys = kd_ref[pl.ds(j0, DENSE_TILES)].reshape(DENSE_TILES * ATT_TILE, PAIR_DIM)
        bias = bias_ref[qi - j0]
        qt = qtd_ref[qi]
        for hd in range(HEADS_PER_PAIR):
            s = jnp.dot(keys, _head_rows(qt, hd), preferred_element_type=_f32) + bias
            s12_ref[t, hd] = s
            m12_ref[t, hd] = jnp.max(s, axis=0, keepdims=True)

    def dense_value(item, t):
        qi = TILES_PER_ITEM * item + t
        j0 = first_tile(qi)
        outs, lses = [], []
        for hd in range(HEADS_PER_PAIR):
            m = m12_ref[t, hd]
            p = jnp.exp2(s12_ref[t, hd] - m)
            vth = jnp.concatenate([vtd_ref[j0 + k, hd * HEAD_DIM:(hd + 1) * HEAD_DIM, :]
                                   for k in range(DENSE_TILES)], axis=1)
            o, e = normalise(jnp.dot(jnp.concatenate([vth, dense_ones], axis=0), p.astype(_bf16),
                                     preferred_element_type=_f32), m)
            outs.append(o)
            lses.append(e)
        o12_ref[_tile_rows(qi), :] = jnp.concatenate(outs, axis=0).T
        e12_ref[_tile_rows(qi), :] = jnp.concatenate(lses, axis=0).T

    def stage(score_item, value_item, score_par):
        value_par = 1 - score_par
        if value_item is not None:
            acc_ref[...] = jnp.zeros(acc_ref.shape, _f32)
        if score_item is not None:
            moba_prepare(score_item, score_par)
        for step in range(n_steps):
            if value_item is not None:
                moba_value_step(value_item, value_par, step)
            if score_item is not None:
                moba_score_step(score_item, score_par, step)
            if step in DENSE_AT_STEP:
                t = DENSE_AT_STEP.index(step)
                if value_item is not None:
                    dense_value(value_item, t)
                if score_item is not None:
                    dense_score(score_item, t)
            if step == CLASS_AT_STEP:
                if value_item is not None:
                    class_value(value_item, value_par)
                if score_item is not None:
                    class_score(score_item, score_par)
            if step == OWN_BLOCK_AT_STEP and value_item is not None:
                moba_value_first(value_item, value_par)
        if value_item is not None:
            moba_finish(value_item)

    _pair_pipeline(nq // 2, stage)

    for t in range(nq):
        rows = slice(t * ATT_TILE, (t + 1) * ATT_TILE)
        e12, e3 = e12_ref[rows, :], e3_ref[rows, :]
        top = jnp.maximum(e12, e3)
        w12 = jnp.exp2(e12 - top)
        w3 = jnp.exp2(e3 - top)
        od_ref[rows, :] = ((w12 * o12_ref[rows, :] + w3 * o3_ref[rows, :]) / (w12 + w3)).astype(od_ref.dtype)


def _mixers(qta, ka, vta, qtd, kd, vtd, aux, bias, band):
    b, n_pairs, nq = qta.shape[:3]
    s = nq * ATT_TILE
    assert nq >= DENSE_TILES and s // SPARSE_DILATION == ATT_TILE
    assert nq // 2 == SPARSE_DILATION // CLASSES_PER_ITEM == nq // TILES_PER_ITEM
    t_spec = pl.BlockSpec((None, None, nq, PAIR_DIM, ATT_TILE), lambda bi, pi: (bi, pi, 0, 0, 0))
    r_spec = pl.BlockSpec((None, None, nq, ATT_TILE, PAIR_DIM), lambda bi, pi: (bi, pi, 0, 0, 0))
    const = lambda a: pl.BlockSpec(a.shape, lambda bi, pi: (0,) * a.ndim, pipeline_mode=pl.Buffered(1))
    o_spec = pl.BlockSpec((None, s, PAIR_DIM), lambda bi, pi: (bi, 0, pi))
    o_shape = jax.ShapeDtypeStruct((b, s, n_pairs * PAIR_DIM), _bf16)
    token_major = lambda: pltpu.VMEM((s, PAIR_DIM), _f32)
    scratch = [
        pltpu.VMEM((nq, PAIR_DIM), _f32),
        pltpu.VMEM((2, HEADS_PER_PAIR, 2 * PAIR_DIM, ATT_TILE), _bf16),
        pltpu.VMEM((2, 2, HEADS_PER_PAIR, ATT_TILE, ATT_TILE), _f32),
        pltpu.VMEM((nq // 2, HEADS_PER_PAIR, 2 * ATT_TILE, ATT_TILE), _f32),
        pltpu.VMEM((2, 2, HEADS_PER_PAIR, 1, ATT_TILE), _f32),
        pltpu.VMEM((2, HEADS_PER_PAIR, HEAD_DIM + ONES_ROWS, ATT_TILE), _f32),
        pltpu.VMEM((TILES_PER_ITEM, HEADS_PER_PAIR, DENSE_TILES * ATT_TILE, ATT_TILE), _f32),
        pltpu.VMEM((TILES_PER_ITEM, HEADS_PER_PAIR, 1, ATT_TILE), _f32),
        token_major(), token_major(), token_major(),
        token_major(), token_major(),
        pltpu.VMEM((2, CLASSES_PER_ITEM, HEADS_PER_PAIR, ATT_TILE, ATT_TILE), _f32),
        pltpu.VMEM((2, CLASSES_PER_ITEM, HEADS_PER_PAIR, 1, ATT_TILE), _f32),
        pltpu.VMEM((2, CLASSES_PER_ITEM, PAIR_DIM, ATT_TILE), _bf16),
        token_major(), token_major(),
    ]
    return pl.pallas_call(
        _mixers_kernel,
        grid=(b, n_pairs),
        in_specs=[t_spec, r_spec, t_spec, const(aux), t_spec, r_spec, t_spec, const(bias), const(band)],
        out_specs=[o_spec, o_spec],
        out_shape=[o_shape, o_shape],
        scratch_shapes=scratch,
        compiler_params=pltpu.CompilerParams(
            dimension_semantics=("parallel", "parallel"), vmem_limit_bytes=VMEM_LIMIT_BYTES),
        name="mixers",
    )(qta, ka, vta, aux, qtd, kd, vtd, bias, band)


def _post_kernel(x_ref, oa_ref, od_ref, ga_ref, gd_ref, wout_ref, gm_ref, wup_ref, wdown_ref,
                 gf_ref, o_ref, *, final_norm):
    da = oa_ref.shape[1]
    na = _rms_norm_rows(oa_ref[...].astype(_f32), ga_ref[...]).astype(_bf16)
    nd = _rms_norm_rows(od_ref[...].astype(_f32), gd_ref[...]).astype(_bf16)
    y = (x_ref[...]
         + jnp.dot(na, wout_ref[:da, :], preferred_element_type=_f32)
         + jnp.dot(nd, wout_ref[da:, :], preferred_element_type=_f32))
    h = _rms_norm_rows(y, gm_ref[...]).astype(_bf16)
    u = jnp.dot(h, wup_ref[...], preferred_element_type=_f32)
    u = jnp.square(jnp.maximum(u, 0.0)).astype(_bf16)
    y = y + jnp.dot(u, wdown_ref[...], preferred_element_type=_f32)
    if final_norm:
        y = _rms_norm_rows(y, gf_ref[...])
    o_ref[...] = y


def _post(x, oa, od, ga, gd, wout_all, gm, wup_all, wdown_all, gf, layer, *, tm, final_norm):
    b, s, d = x.shape
    da = oa.shape[-1]
    dd = od.shape[-1]
    row = lambda bi, si: (bi, si, 0)
    const = lambda bi, si: (0, 0)
    whole = lambda a: pl.BlockSpec((None,) + a.shape[1:], lambda bi, si: (layer, 0, 0),
                                   pipeline_mode=pl.Buffered(1))
    return pl.pallas_call(
        functools.partial(_post_kernel, final_norm=final_norm),
        grid=(b, s // tm),
        in_specs=[
            pl.BlockSpec((None, tm, d), row),
            pl.BlockSpec((None, tm, da), row),
            pl.BlockSpec((None, tm, dd), row),
            pl.BlockSpec((1, da), const),
            pl.BlockSpec((1, dd), const),
            whole(wout_all),
            pl.BlockSpec((1, d), const),
            whole(wup_all),
            whole(wdown_all),
            pl.BlockSpec((1, d), const),
        ],
        out_specs=pl.BlockSpec((None, tm, d), row),
        out_shape=jax.ShapeDtypeStruct((b, s, d), _f32),
        compiler_params=pltpu.CompilerParams(
            dimension_semantics=("parallel", "parallel"), vmem_limit_bytes=VMEM_LIMIT_BYTES),
        name="post",
    )(x, oa, od, ga, gd, wout_all, gm, wup_all, wdown_all, gf)


def _rope_tables(seq):
    inv = 1.0 / (ROPE_THETA ** (jnp.arange(0, HEAD_DIM, 2, dtype=_f32) / HEAD_DIM))
    ang = jnp.arange(seq, dtype=_f32)[:, None] * inv[None, :]
    cos, sin = jnp.cos(ang), jnp.sin(ang)
    reps = PAIR_DIM // (HEAD_DIM // 2)
    cos_t = jnp.tile(cos, (1, reps))
    sin_t = jnp.tile(jnp.concatenate([-sin, sin], axis=-1), (1, HEADS_PER_PAIR))
    return cos_t, sin_t


def kernel(x, attn_norm, w_in, moba_out_norm, dil_out_norm, w_out, mlp_norm, w_up, w_down, final_norm):
    depth = w_in.shape[0]
    seq = x.shape[1]
    cos_t, sin_t = _rope_tables(seq)
    aux = _moba_block_onehot(seq // ATT_TILE)
    bias = _dense_bias_table()
    band = _class_band_table()
    row = lambda v: v.reshape(1, -1)
    w_in, w_out, w_up, w_down = (w.astype(_bf16) for w in (w_in, w_out, w_up, w_down))
    for l in range(depth):
        qta, ka, vta, qtd, kd, vtd = _inproj(x, row(attn_norm[l]), w_in, l, cos_t, sin_t, tm=512)
        out_a, out_d = _mixers(qta, ka, vta, qtd, kd, vtd, aux, bias, band)
        x = _post(x, out_a, out_d, row(moba_out_norm[l]), row(dil_out_norm[l]), w_out, row(mlp_norm[l]),
                  w_up, w_down, row(final_norm), l, tm=512, final_norm=(l == depth - 1))
    return x
```

```python
import functools
import math

import jax
import jax.numpy as jnp
from jax import lax
from jax.experimental import pallas as pl
from jax.experimental.pallas import tpu as pltpu

HEAD_DIM = 64
HEADS_PER_PAIR = 2
PAIR_DIM = HEADS_PER_PAIR * HEAD_DIM
MOBA_BLOCK = 256
MOBA_TOPK = 3
DILATED_CONFIGS = ((128, 1), (512, 4), (2048, 16))
ROPE_THETA = 10000.0
NORM_EPS = 1e-6
NEG_INF = -1e30
ATT_TILE = 256
ONES_ROWS = 16
VMEM_LIMIT_BYTES = 56 * 1024 * 1024

_f32 = jnp.float32
_bf16 = jnp.bfloat16


def _rms_norm_rows(x, g):
    return x * lax.rsqrt(jnp.mean(x * x, axis=-1, keepdims=True) + NORM_EPS) * g


def _inproj_kernel(x_ref, g_ref, w_ref, cos_ref, sin_ref,
                   qta_ref, ka_ref, vta_ref, qtd_ref, kd_ref, vtd_ref, *, n_pairs, scale):
    tm = x_ref.shape[0]
    nt = tm // ATT_TILE
    h = _rms_norm_rows(x_ref[...], g_ref[...]).astype(_bf16)
    group = n_pairs * PAIR_DIM
    cos = cos_ref[...]
    sin = sin_ref[...]
    lane = lax.broadcasted_iota(jnp.int32, (tm, PAIR_DIM), 1)
    first_half = (lane % HEAD_DIM) < (HEAD_DIM // 2)

    def rope(c):
        partner = jnp.where(first_half,
                            pltpu.roll(c, PAIR_DIM - HEAD_DIM // 2, 1),
                            pltpu.roll(c, HEAD_DIM // 2, 1))
        return c * cos + partner * sin

    def project(g):
        return jnp.dot(h, w_ref[:, g * group:(g + 1) * group], preferred_element_type=_f32)

    def put_transposed(ref, p, c):
        for t in range(nt):
            ref[p, t] = c[t * ATT_TILE:(t + 1) * ATT_TILE, :].T.astype(_bf16)

    def put_rows(ref, p, c):
        for t in range(nt):
            ref[p, t] = c[t * ATT_TILE:(t + 1) * ATT_TILE, :].astype(_bf16)

    plan = ((qta_ref, put_transposed, lambda c: rope(c) * scale), (ka_ref, put_rows, rope),
            (vta_ref, put_transposed, lambda c: c),
            (qtd_ref, put_transposed, lambda c: rope(c) * scale), (kd_ref, put_rows, rope),
            (vtd_ref, put_transposed, lambda c: c))
    for g, (ref, put, fn) in enumerate(plan):
        proj = project(g)
        for p in range(n_pairs):
            put(ref, p, fn(proj[:, p * PAIR_DIM:(p + 1) * PAIR_DIM]))


def _inproj(x, g, w_all, layer, cos_t, sin_t, *, tm):
    b, s, d = x.shape
    n_pairs = w_all.shape[2] // (6 * PAIR_DIM)
    nq = s // ATT_TILE
    nt = tm // ATT_TILE
    t_shape = jax.ShapeDtypeStruct((b, n_pairs, nq, PAIR_DIM, ATT_TILE), _bf16)
    r_shape = jax.ShapeDtypeStruct((b, n_pairs, nq, ATT_TILE, PAIR_DIM), _bf16)
    t_spec = pl.BlockSpec((None, n_pairs, nt, PAIR_DIM, ATT_TILE), lambda bi, si: (bi, 0, si, 0, 0))
    r_spec = pl.BlockSpec((None, n_pairs, nt, ATT_TILE, PAIR_DIM), lambda bi, si: (bi, 0, si, 0, 0))
    return pl.pallas_call(
        functools.partial(_inproj_kernel, n_pairs=n_pairs, scale=HEAD_DIM ** -0.5 * math.log2(math.e)),
        grid=(b, s // tm),
        in_specs=[
            pl.BlockSpec((None, tm, d), lambda bi, si: (bi, si, 0)),
            pl.BlockSpec((1, d), lambda bi, si: (0, 0)),
            pl.BlockSpec((None,) + w_all.shape[1:], lambda bi, si: (layer, 0, 0), pipeline_mode=pl.Buffered(1)),
            pl.BlockSpec((tm, PAIR_DIM), lambda bi, si: (si, 0)),
            pl.BlockSpec((tm, PAIR_DIM), lambda bi, si: (si, 0)),
        ],
        out_specs=[t_spec, r_spec, t_spec, t_spec, r_spec, t_spec],
        out_shape=[t_shape, r_shape, t_shape, t_shape, r_shape, t_shape],
        compiler_params=pltpu.CompilerParams(
            dimension_semantics=("parallel", "parallel"), vmem_limit_bytes=VMEM_LIMIT_BYTES),
        name="inproj",
    )(x, g, w_all, cos_t, sin_t)


def _head_rows(qt, head):
    row = lax.broadcasted_iota(jnp.int32, qt.shape, 0)
    mine = (row >= head * HEAD_DIM) & (row < (head + 1) * HEAD_DIM)
    return jnp.where(mine, qt, jnp.zeros_like(qt))


def _pair_pipeline(n_pairs, stage):
    assert n_pairs >= 2 and n_pairs % 2 == 0
    stage(0, None, 0)

    def two_pairs(u, carry):
        stage(2 * u + 1, 2 * u, 1)
        stage(2 * u + 2, 2 * u + 1, 0)
        return carry

    lax.fori_loop(0, n_pairs // 2 - 1, two_pairs, 0)
    stage(n_pairs - 1, n_pairs - 2, 1)
    stage(None, n_pairs - 1, 0)


DENSE_CONFIGS = DILATED_CONFIGS[:2]
SPARSE_WINDOW, SPARSE_DILATION = DILATED_CONFIGS[2]
DENSE_TILES = max(w for w, _ in DENSE_CONFIGS) // ATT_TILE + 1
CLASSES_PER_ITEM = 2
TILES_PER_ITEM = 2
DENSE_AT_STEP = (1, 5)
CLASS_AT_STEP = 3
OWN_BLOCK_AT_STEP = 6


def _moba_block_onehot(nq):
    c = jnp.arange(nq // 2, dtype=jnp.int32)[:, None, None]
    r = jnp.arange(2 * ATT_TILE, dtype=jnp.int32)[None, :, None]
    lane = jnp.arange(PAIR_DIM, dtype=jnp.int32)[None, None, :]
    return (lane == 2 * c + r // ATT_TILE).astype(_bf16)


def _dense_bias_table():
    v = jnp.arange(DENSE_TILES, dtype=jnp.int32)[:, None, None]
    row = jnp.arange(DENSE_TILES * ATT_TILE, dtype=jnp.int32)[None, :, None]
    qry = jnp.arange(ATT_TILE, dtype=jnp.int32)[None, None, :]
    delta = v * ATT_TILE + qry - row
    count = jnp.zeros(delta.shape, _f32)
    for window, dilation in DENSE_CONFIGS:
        hit = (delta >= 0) & (delta <= window) & (delta % dilation == 0)
        count = count + hit.astype(_f32)
    return jnp.where(count > 0, jnp.log2(jnp.maximum(count, 1.0)), NEG_INF)


def _class_band_table():
    key = jnp.arange(ATT_TILE, dtype=jnp.int32)[:, None]
    qry = jnp.arange(ATT_TILE, dtype=jnp.int32)[None, :]
    delta = qry - key
    return jnp.where((delta >= 0) & (delta <= SPARSE_WINDOW // SPARSE_DILATION), 0.0, NEG_INF).astype(_f32)


def _tile_rows(qi):
    row0 = qi * ATT_TILE
    if not isinstance(row0, int):
        row0 = pl.multiple_of(row0, ATT_TILE)
    return pl.ds(row0, ATT_TILE)


def _mixers_kernel(qta_ref, ka_ref, vta_ref, aux_ref, qtd_ref, kd_ref, vtd_ref, bias_ref, band_ref,
                   oa_ref, od_ref,
                   kmean_ref, w_ref, sd_ref, sa_ref, ma_ref, acc_ref,
                   s12_ref, m12_ref, xq_ref, xk_ref, xv_ref, o3_ref, e3_ref, s3_ref, m3_ref, vt3_ref,
                   o12_ref, e12_ref):
    nq = qta_ref.shape[0]
    n_steps = nq // 2
    chunk = 2 * ATT_TILE
    ones = jnp.ones((ONES_ROWS, ATT_TILE), _bf16)

    for j in range(nq):
        kmean_ref[j:j + 1, :] = jnp.mean(ka_ref[j].astype(_f32), axis=0, keepdims=True)
    km = kmean_ref[...]
    km_hi = km.astype(_bf16)
    km_lo = (km - km_hi.astype(_f32)).astype(_bf16)
    for slot in range(2):
        for hd in range(HEADS_PER_PAIR):
            w_ref[slot, hd, PAIR_DIM + nq:, :] = jnp.zeros((PAIR_DIM - nq, ATT_TILE), _bf16)

    key_i = lax.broadcasted_iota(jnp.int32, (ATT_TILE, ATT_TILE), 0)
    qry_i = lax.broadcasted_iota(jnp.int32, (ATT_TILE, ATT_TILE), 1)
    causal = key_i <= qry_i
    blk = lax.broadcasted_iota(jnp.int32, (nq, ATT_TILE), 0)
    blk_f = blk.astype(_f32)

    def moba_vt(j, hd):
        return jnp.concatenate([vta_ref[j, hd * HEAD_DIM:(hd + 1) * HEAD_DIM, :], ones], axis=0)

    def tile_of(pair, slot):
        return pair if slot == 0 else nq - 1 - pair

    def plan(pair, step):
        n_first = (pair + 1) // 2
        first = step < n_first
        return jnp.where(first, 0, 1), jnp.where(first, step, step - n_first)

    def moba_prepare(pair, par):
        for slot in range(2):
            qi = tile_of(pair, slot)
            qt = qta_ref[qi]
            past = blk < qi
            for hd in range(HEADS_PER_PAIR):
                qth = _head_rows(qt, hd)
                gate = (jnp.dot(km_hi, qth, preferred_element_type=_f32)
                        + jnp.dot(km_lo, qth, preferred_element_type=_f32))
                gate = jnp.where(past, gate, NEG_INF)
                chosen = jnp.zeros(gate.shape, _f32)
                for _ in range(MOBA_TOPK):
                    top = jnp.max(gate, axis=0, keepdims=True)
                    first = jnp.min(jnp.where(gate == top, blk_f, float(nq)), axis=0, keepdims=True)
                    pick = blk_f == first
                    chosen = jnp.where(pick, 1.0, chosen)
                    gate = jnp.where(pick, -jnp.inf, gate)
                sel = jnp.where(past, jnp.where(chosen > 0.0, 0.0, NEG_INF), NEG_INF)
                w_ref[slot, hd, :PAIR_DIM, :] = qth
                w_ref[slot, hd, PAIR_DIM:PAIR_DIM + nq, :] = sel.astype(_bf16)
                s = jnp.dot(ka_ref[qi], qth, preferred_element_type=_f32)
                s = jnp.where(causal, s, NEG_INF)
                sd_ref[par, slot, hd] = s
                ma_ref[par, slot, hd] = jnp.max(s, axis=0, keepdims=True)

    def moba_score_step(pair, par, step):
        slot, c = plan(pair, step)
        keys = ka_ref[pl.ds(2 * c, 2)].reshape(chunk, PAIR_DIM)
        lhs = jnp.concatenate([keys, aux_ref[c]], axis=1)
        for hd in range(HEADS_PER_PAIR):
            s = jnp.dot(lhs, w_ref[slot, hd], preferred_element_type=_f32)
            sa_ref[step, hd] = s
            ma_ref[par, slot, hd] = jnp.maximum(ma_ref[par, slot, hd], jnp.max(s, axis=0, keepdims=True))

    def moba_value_first(pair, par):
        for slot in range(2):
            qi = tile_of(pair, slot)
            for hd in range(HEADS_PER_PAIR):
                p = jnp.exp2(sd_ref[par, slot, hd] - ma_ref[par, slot, hd])
                acc_ref[slot, hd] += jnp.dot(moba_vt(qi, hd), p.astype(_bf16), preferred_element_type=_f32)

    def moba_value_step(pair, par, step):
        slot, c = plan(pair, step)
        for hd in range(HEADS_PER_PAIR):
            p = jnp.exp2(sa_ref[step, hd] - ma_ref[par, slot, hd])
            vtc = jnp.concatenate([moba_vt(2 * c, hd), moba_vt(2 * c + 1, hd)], axis=1)
            acc_ref[slot, hd] += jnp.dot(vtc, p.astype(_bf16), preferred_element_type=_f32)

    def moba_finish(pair):
        for slot in range(2):
            o_t = jnp.concatenate([acc_ref[slot, hd, :HEAD_DIM, :] / acc_ref[slot, hd, HEAD_DIM:HEAD_DIM + 1, :]
                                   for hd in range(HEADS_PER_PAIR)], axis=0)
            oa_ref[_tile_rows(tile_of(pair, slot)), :] = o_t.T.astype(oa_ref.dtype)

    for t in range(nq):
        rows = slice(t * ATT_TILE, (t + 1) * ATT_TILE)
        xq_ref[rows, :] = qtd_ref[t].astype(_f32).T
        xk_ref[rows, :] = kd_ref[t].astype(_f32)
        xv_ref[rows, :] = vtd_ref[t].astype(_f32).T

    def members(item, c):
        return pl.ds(CLASSES_PER_ITEM * item + c, ATT_TILE, stride=SPARSE_DILATION)

    def normalise(acc, m):
        l = acc[HEAD_DIM:HEAD_DIM + 1, :]
        return acc[:HEAD_DIM, :] / l, jnp.broadcast_to(m + jnp.log2(l), (HEAD_DIM, ATT_TILE))

    def class_score(item, par):
        band = band_ref[...]
        for c in range(CLASSES_PER_ITEM):
            qt = xq_ref[members(item, c), :].T.astype(_bf16)
            keys = xk_ref[members(item, c), :].astype(_bf16)
            vt3_ref[par, c] = xv_ref[members(item, c), :].T.astype(_bf16)
            for hd in range(HEADS_PER_PAIR):
                s = jnp.dot(keys, _head_rows(qt, hd), preferred_element_type=_f32) + band
                s3_ref[par, c, hd] = s
                m3_ref[par, c, hd] = jnp.max(s, axis=0, keepdims=True)

    def class_value(item, par):
        for c in range(CLASSES_PER_ITEM):
            outs, lses = [], []
            for hd in range(HEADS_PER_PAIR):
                m = m3_ref[par, c, hd]
                p = jnp.exp2(s3_ref[par, c, hd] - m)
                vth = jnp.concatenate([vt3_ref[par, c, hd * HEAD_DIM:(hd + 1) * HEAD_DIM, :], ones], axis=0)
                o, e = normalise(jnp.dot(vth, p.astype(_bf16), preferred_element_type=_f32), m)
                outs.append(o)
                lses.append(e)
            o3_ref[members(item, c), :] = jnp.concatenate(outs, axis=0).T
            e3_ref[members(item, c), :] = jnp.concatenate(lses, axis=0).T

    dense_ones = jnp.ones((ONES_ROWS, DENSE_TILES * ATT_TILE), _bf16)

    def first_tile(qi):
        return jnp.maximum(qi - (DENSE_TILES - 1), 0)

    def dense_score(item, t):
        qi = TILES_PER_ITEM * item + t
        j0 = first_tile(qi)
        keys = kd_ref[pl.ds(j0, DENSE_TILES)].reshape(DENSE_TILES * ATT_TILE, PAIR_DIM)
        bias = bias_ref[qi - j0]
        qt = qtd_ref[qi]
        for hd in range(HEADS_PER_PAIR):
            s = jnp.dot(keys, _head_rows(qt, hd), preferred_element_type=_f32) + bias
            s12_ref[t, hd] = s
            m12_ref[t, hd] = jnp.max(s, axis=0, keepdims=True)

    def dense_value(item, t):
        qi = TILES_PER_ITEM * item + t
        j0 = first_tile(qi)
        outs, lses = [], []
        for hd in range(HEADS_PER_PAIR):
            m = m12_ref[t, hd]
            p = jnp.exp2(s12_ref[t, hd] - m)
            vth = jnp.concatenate([vtd_ref[j0 + k, hd * HEAD_DIM:(hd + 1) * HEAD_DIM, :]
                                   for k in range(DENSE_TILES)], axis=1)
            o, e = normalise(jnp.dot(jnp.concatenate([vth, dense_ones], axis=0), p.astype(_bf16),
                                     preferred_element_type=_f32), m)
            outs.append(o)
            lses.append(e)
        o12_ref[_tile_rows(qi), :] = jnp.concatenate(outs, axis=0).T
        e12_ref[_tile_rows(qi), :] = jnp.concatenate(lses, axis=0).T

    def stage(score_item, value_item, score_par):
        value_par = 1 - score_par
        if value_item is not None:
            acc_ref[...] = jnp.zeros(acc_ref.shape, _f32)
        if score_item is not None:
            moba_prepare(score_item, score_par)
        for step in range(n_steps):
            if value_item is not None:
                moba_value_step(value_item, value_par, step)
            if score_item is not None:
                moba_score_step(score_item, score_par, step)
            if step in DENSE_AT_STEP:
                t = DENSE_AT_STEP.index(step)
                if value_item is not None:
                    dense_value(value_item, t)
                if score_item is not None:
                    dense_score(score_item, t)
            if step == CLASS_AT_STEP:
                if value_item is not None:
                    class_value(value_item, value_par)
                if score_item is not None:
                    class_score(score_item, score_par)
            if step == OWN_BLOCK_AT_STEP and value_item is not None:
                moba_value_first(value_item, value_par)
        if value_item is not None:
            moba_finish(value_item)

    _pair_pipeline(nq // 2, stage)

    for t in range(nq):
        rows = slice(t * ATT_TILE, (t + 1) * ATT_TILE)
        e12, e3 = e12_ref[rows, :], e3_ref[rows, :]
        top = jnp.maximum(e12, e3)
        w12 = jnp.exp2(e12 - top)
        w3 = jnp.exp2(e3 - top)
        od_ref[rows, :] = ((w12 * o12_ref[rows, :] + w3 * o3_ref[rows, :]) / (w12 + w3)).astype(od_ref.dtype)


def _mixers(qta, ka, vta, qtd, kd, vtd, aux, bias, band):
    b, n_pairs, nq = qta.shape[:3]
    s = nq * ATT_TILE
    assert nq >= DENSE_TILES and s // SPARSE_DILATION == ATT_TILE
    assert nq // 2 == SPARSE_DILATION // CLASSES_PER_ITEM == nq // TILES_PER_ITEM
    t_spec = pl.BlockSpec((None, None, nq, PAIR_DIM, ATT_TILE), lambda bi, pi: (bi, pi, 0, 0, 0))
    r_spec = pl.BlockSpec((None, None, nq, ATT_TILE, PAIR_DIM), lambda bi, pi: (bi, pi, 0, 0, 0))
    const = lambda a: pl.BlockSpec(a.shape, lambda bi, pi: (0,) * a.ndim, pipeline_mode=pl.Buffered(1))
    o_spec = pl.BlockSpec((None, s, PAIR_DIM), lambda bi, pi: (bi, 0, pi))
    o_shape = jax.ShapeDtypeStruct((b, s, n_pairs * PAIR_DIM), _bf16)
    token_major = lambda: pltpu.VMEM((s, PAIR_DIM), _f32)
    scratch = [
        pltpu.VMEM((nq, PAIR_DIM), _f32),
        pltpu.VMEM((2, HEADS_PER_PAIR, 2 * PAIR_DIM, ATT_TILE), _bf16),
        pltpu.VMEM((2, 2, HEADS_PER_PAIR, ATT_TILE, ATT_TILE), _f32),
        pltpu.VMEM((nq // 2, HEADS_PER_PAIR, 2 * ATT_TILE, ATT_TILE), _f32),
        pltpu.VMEM((2, 2, HEADS_PER_PAIR, 1, ATT_TILE), _f32),
        pltpu.VMEM((2, HEADS_PER_PAIR, HEAD_DIM + ONES_ROWS, ATT_TILE), _f32),
        pltpu.VMEM((TILES_PER_ITEM, HEADS_PER_PAIR, DENSE_TILES * ATT_TILE, ATT_TILE), _f32),
        pltpu.VMEM((TILES_PER_ITEM, HEADS_PER_PAIR, 1, ATT_TILE), _f32),
        token_major(), token_major(), token_major(),
        token_major(), token_major(),
        pltpu.VMEM((2, CLASSES_PER_ITEM, HEADS_PER_PAIR, ATT_TILE, ATT_TILE), _f32),
        pltpu.VMEM((2, CLASSES_PER_ITEM, HEADS_PER_PAIR, 1, ATT_TILE), _f32),
        pltpu.VMEM((2, CLASSES_PER_ITEM, PAIR_DIM, ATT_TILE), _bf16),
        token_major(), token_major(),
    ]
    return pl.pallas_call(
        _mixers_kernel,
        grid=(b, n_pairs),
        in_specs=[t_spec, r_spec, t_spec, const(aux), t_spec, r_spec, t_spec, const(bias), const(band)],
        out_specs=[o_spec, o_spec],
        out_shape=[o_shape, o_shape],
        scratch_shapes=scratch,
        compiler_params=pltpu.CompilerParams(
            dimension_semantics=("parallel", "parallel"), vmem_limit_bytes=VMEM_LIMIT_BYTES),
        name="mixers",
    )(qta, ka, vta, aux, qtd, kd, vtd, bias, band)


def _post_kernel(x_ref, oa_ref, od_ref, ga_ref, gd_ref, wout_ref, gm_ref, wup_ref, wdown_ref,
                 gf_ref, o_ref, *, final_norm):
    da = oa_ref.shape[1]
    na = _rms_norm_rows(oa_ref[...].astype(_f32), ga_ref[...]).astype(_bf16)
    nd = _rms_norm_rows(od_ref[...].astype(_f32), gd_ref[...]).astype(_bf16)
    y = (x_ref[...]
         + jnp.dot(na, wout_ref[:da, :], preferred_element_type=_f32)
         + jnp.dot(nd, wout_ref[da:, :], preferred_element_type=_f32))
    h = _rms_norm_rows(y, gm_ref[...]).astype(_bf16)
    u = jnp.dot(h, wup_ref[...], preferred_element_type=_f32)
    u = jnp.square(jnp.maximum(u, 0.0)).astype(_bf16)
    y = y + jnp.dot(u, wdown_ref[...], preferred_element_type=_f32)
    if final_norm:
        y = _rms_norm_rows(y, gf_ref[...])
    o_ref[...] = y


def _post(x, oa, od, ga, gd, wout_all, gm, wup_all, wdown_all, gf, layer, *, tm, final_norm):
    b, s, d = x.shape
    da = oa.shape[-1]
    dd = od.shape[-1]
    row = lambda bi, si: (bi, si, 0)
    const = lambda bi, si: (0, 0)
    whole = lambda a: pl.BlockSpec((None,) + a.shape[1:], lambda bi, si: (layer, 0, 0),
                                   pipeline_mode=pl.Buffered(1))
    return pl.pallas_call(
        functools.partial(_post_kernel, final_norm=final_norm),
        grid=(b, s // tm),
        in_specs=[
            pl.BlockSpec((None, tm, d), row),
            pl.BlockSpec((None, tm, da), row),
            pl.BlockSpec((None, tm, dd), row),
            pl.BlockSpec((1, da), const),
            pl.BlockSpec((1, dd), const),
            whole(wout_all),
            pl.BlockSpec((1, d), const),
            whole(wup_all),
            whole(wdown_all),
            pl.BlockSpec((1, d), const),
        ],
        out_specs=pl.BlockSpec((None, tm, d), row),
        out_shape=jax.ShapeDtypeStruct((b, s, d), _f32),
        compiler_params=pltpu.CompilerParams(
            dimension_semantics=("parallel", "parallel"), vmem_limit_bytes=VMEM_LIMIT_BYTES),
        name="post",
    )(x, oa, od, ga, gd, wout_all, gm, wup_all, wdown_all, gf)


def _rope_tables(seq):
    inv = 1.0 / (ROPE_THETA ** (jnp.arange(0, HEAD_DIM, 2, dtype=_f32) / HEAD_DIM))
    ang = jnp.arange(seq, dtype=_f32)[:, None] * inv[None, :]
    cos, sin = jnp.cos(ang), jnp.sin(ang)
    reps = PAIR_DIM // (HEAD_DIM // 2)
    cos_t = jnp.tile(cos, (1, reps))
    sin_t = jnp.tile(jnp.concatenate([-sin, sin], axis=-1), (1, HEADS_PER_PAIR))
    return cos_t, sin_t


def kernel(x, attn_norm, w_in, moba_out_norm, dil_out_norm, w_out, mlp_norm, w_up, w_down, final_norm):
    depth = w_in.shape[0]
    seq = x.shape[1]
    cos_t, sin_t = _rope_tables(seq)
    aux = _moba_block_onehot(seq // ATT_TILE)
    bias = _dense_bias_table()
    band = _class_band_table()
    row = lambda v: v.reshape(1, -1)
    w_in, w_out, w_up, w_down = (w.astype(_bf16) for w in (w_in, w_out, w_up, w_down))
    for l in range(depth):
        qta, ka, vta, qtd, kd, vtd = _inproj(x, row(attn_norm[l]), w_in, l, cos_t, sin_t, tm=512)
        out_a, out_d = _mixers(qta, ka, vta, qtd, kd, vtd, aux, bias, band)
        x = _post(x, out_a, out_d, row(moba_out_norm[l]), row(dil_out_norm[l]), w_out, row(mlp_norm[l]),
                  w_up, w_down, row(final_norm), l, tm=512, final_norm=(l == depth - 1))
    return x
```

```python
import functools
import math

import jax
import jax.numpy as jnp
from jax import lax
from jax.experimental import pallas as pl
from jax.experimental.pallas import tpu as pltpu

HEAD_DIM = 64
HEADS_PER_PAIR = 2
PAIR_DIM = HEADS_PER_PAIR * HEAD_DIM
MOBA_BLOCK = 256
MOBA_TOPK = 3
DILATED_CONFIGS = ((128, 1), (512, 4), (2048, 16))
ROPE_THETA = 10000.0
NORM_EPS = 1e-6
NEG_INF = -1e30
ATT_TILE = 256
ONES_ROWS = 16
VMEM_LIMIT_BYTES = 56 * 1024 * 1024

_f32 = jnp.float32
_bf16 = jnp.bfloat16


def _rms_norm_rows(x, g):
    return x * lax.rsqrt(jnp.mean(x * x, axis=-1, keepdims=True) + NORM_EPS) * g


def _inproj_kernel(x_ref, g_ref, w_ref, cos_ref, sin_ref,
                   qta_ref, ka_ref, vta_ref, qtd_ref, kd_ref, vtd_ref, *, n_pairs, scale):
    tm = x_ref.shape[0]
    nt = tm // ATT_TILE
    h = _rms_norm_rows(x_ref[...], g_ref[...]).astype(_bf16)
    group = n_pairs * PAIR_DIM
    cos = cos_ref[...]
    sin = sin_ref[...]
    lane = lax.broadcasted_iota(jnp.int32, (tm, PAIR_DIM), 1)
    first_half = (lane % HEAD_DIM) < (HEAD_DIM // 2)

    def rope(c):
        partner = jnp.where(first_half,
                            pltpu.roll(c, PAIR_DIM - HEAD_DIM // 2, 1),
                            pltpu.roll(c, HEAD_DIM // 2, 1))
        return c * cos + partner * sin

    def project(g):
        return jnp.dot(h, w_ref[:, g * group:(g + 1) * group], preferred_element_type=_f32)

    def put_transposed(ref, p, c):
        for t in range(nt):
            ref[p, t] = c[t * ATT_TILE:(t + 1) * ATT_TILE, :].T.astype(_bf16)

    def put_rows(ref, p, c):
        for t in range(nt):
            ref[p, t] = c[t * ATT_TILE:(t + 1) * ATT_TILE, :].astype(_bf16)

    plan = ((qta_ref, put_transposed, lambda c: rope(c) * scale), (ka_ref, put_rows, rope),
            (vta_ref, put_transposed, lambda c: c),
            (qtd_ref, put_transposed, lambda c: rope(c) * scale), (kd_ref, put_rows, rope),
            (vtd_ref, put_transposed, lambda c: c))
    for g, (ref, put, fn) in enumerate(plan):
        proj = project(g)
        for p in range(n_pairs):
            put(ref, p, fn(proj[:, p * PAIR_DIM:(p + 1) * PAIR_DIM]))


def _inproj(x, g, w_all, layer, cos_t, sin_t, *, tm):
    b, s, d = x.shape
    n_pairs = w_all.shape[2] // (6 * PAIR_DIM)
    nq = s // ATT_TILE
    nt = tm // ATT_TILE
    t_shape = jax.ShapeDtypeStruct((b, n_pairs, nq, PAIR_DIM, ATT_TILE), _bf16)
    r_shape = jax.ShapeDtypeStruct((b, n_pairs, nq, ATT_TILE, PAIR_DIM), _bf16)
    t_spec = pl.BlockSpec((None, n_pairs, nt, PAIR_DIM, ATT_TILE), lambda bi, si: (bi, 0, si, 0, 0))
    r_spec = pl.BlockSpec((None, n_pairs, nt, ATT_TILE, PAIR_DIM), lambda bi, si: (bi, 0, si, 0, 0))
    return pl.pallas_call(
        functools.partial(_inproj_kernel, n_pairs=n_pairs, scale=HEAD_DIM ** -0.5 * math.log2(math.e)),
        grid=(b, s // tm),
        in_specs=[
            pl.BlockSpec((None, tm, d), lambda bi, si: (bi, si, 0)),
            pl.BlockSpec((1, d), lambda bi, si: (0, 0)),
            pl.BlockSpec((None,) + w_all.shape[1:], lambda bi, si: (layer, 0, 0), pipeline_mode=pl.Buffered(1)),
            pl.BlockSpec((tm, PAIR_DIM), lambda bi, si: (si, 0)),
            pl.BlockSpec((tm, PAIR_DIM), lambda bi, si: (si, 0)),
        ],
        out_specs=[t_spec, r_spec, t_spec, t_spec, r_spec, t_spec],
        out_shape=[t_shape, r_shape, t_shape, t_shape, r_shape, t_shape],
        compiler_params=pltpu.CompilerParams(
            dimension_semantics=("parallel", "parallel"), vmem_limit_bytes=VMEM_LIMIT_BYTES),
        name="inproj",
    )(x, g, w_all, cos_t, sin_t)


def _head_rows(qt, head):
    row = lax.broadcasted_iota(jnp.int32, qt.shape, 0)
    mine = (row >= head * HEAD_DIM) & (row < (head + 1) * HEAD_DIM)
    return jnp.where(mine, qt, jnp.zeros_like(qt))


def _pair_pipeline(n_pairs, stage):
    assert n_pairs >= 2 and n_pairs % 2 == 0
    stage(0, None, 0)

    def two_pairs(u, carry):
        stage(2 * u + 1, 2 * u, 1)
        stage(2 * u + 2, 2 * u + 1, 0)
        return carry

    lax.fori_loop(0, n_pairs // 2 - 1, two_pairs, 0)
    stage(n_pairs - 1, n_pairs - 2, 1)
    stage(None, n_pairs - 1, 0)


DENSE_CONFIGS = DILATED_CONFIGS[:2]
SPARSE_WINDOW, SPARSE_DILATION = DILATED_CONFIGS[2]
DENSE_TILES = max(w for w, _ in DENSE_CONFIGS) // ATT_TILE + 1
CLASSES_PER_ITEM = 2
TILES_PER_ITEM = 2
DENSE_AT_STEP = (1, 5)
CLASS_AT_STEP = 3
OWN_BLOCK_AT_STEP = 6


def _moba_block_onehot(nq):
    c = jnp.arange(nq // 2, dtype=jnp.int32)[:, None, None]
    r = jnp.arange(2 * ATT_TILE, dtype=jnp.int32)[None, :, None]
    lane = jnp.arange(PAIR_DIM, dtype=jnp.int32)[None, None, :]
    return (lane == 2 * c + r // ATT_TILE).astype(_bf16)


def _dense_bias_table():
    v = jnp.arange(DENSE_TILES, dtype=jnp.int32)[:, None, None]
    row = jnp.arange(DENSE_TILES * ATT_TILE, dtype=jnp.int32)[None, :, None]
    qry = jnp.arange(ATT_TILE, dtype=jnp.int32)[None, None, :]
    delta = v * ATT_TILE + qry - row
    count = jnp.zeros(delta.shape, _f32)
    for window, dilation in DENSE_CONFIGS:
        hit = (delta >= 0) & (delta <= window) & (delta % dilation == 0)
        count = count + hit.astype(_f32)
    return jnp.where(count > 0, jnp.log2(jnp.maximum(count, 1.0)), NEG_INF)


def _class_band_table():
    key = jnp.arange(ATT_TILE, dtype=jnp.int32)[:, None]
    qry = jnp.arange(ATT_TILE, dtype=jnp.int32)[None, :]
    delta = qry - key
    return jnp.where((delta >= 0) & (delta <= SPARSE_WINDOW // SPARSE_DILATION), 0.0, NEG_INF).astype(_f32)


def _tile_rows(qi):
    row0 = qi * ATT_TILE
    if not isinstance(row0, int):
        row0 = pl.multiple_of(row0, ATT_TILE)
    return pl.ds(row0, ATT_TILE)


def _mixers_kernel(qta_ref, ka_ref, vta_ref, aux_ref, qtd_ref, kd_ref, vtd_ref, bias_ref, band_ref,
                   oa_ref, od_ref,
                   kmean_ref, w_ref, sd_ref, sa_ref, ma_ref, acc_ref,
                   s12_ref, m12_ref, xq_ref, xk_ref, xv_ref, o3_ref, e3_ref, s3_ref, m3_ref, vt3_ref,
                   o12_ref, e12_ref):
    nq = qta_ref.shape[0]
    n_steps = nq // 2
    chunk = 2 * ATT_TILE
    ones = jnp.ones((ONES_ROWS, ATT_TILE), _bf16)

    for j in range(nq):
        kmean_ref[j:j + 1, :] = jnp.mean(ka_ref[j].astype(_f32), axis=0, keepdims=True)
    km = kmean_ref[...]
    km_hi = km.astype(_bf16)
    km_lo = (km - km_hi.astype(_f32)).astype(_bf16)
    for slot in range(2):
        for hd in range(HEADS_PER_PAIR):
            w_ref[slot, hd, PAIR_DIM + nq:, :] = jnp.zeros((PAIR_DIM - nq, ATT_TILE), _bf16)

    key_i = lax.broadcasted_iota(jnp.int32, (ATT_TILE, ATT_TILE), 0)
    qry_i = lax.broadcasted_iota(jnp.int32, (ATT_TILE, ATT_TILE), 1)
    causal = key_i <= qry_i
    blk = lax.broadcasted_iota(jnp.int32, (nq, ATT_TILE), 0)
    blk_f = blk.astype(_f32)

    def moba_vt(j, hd):
        return jnp.concatenate([vta_ref[j, hd * HEAD_DIM:(hd + 1) * HEAD_DIM, :], ones], axis=0)

    def tile_of(pair, slot):
        return pair if slot == 0 else nq - 1 - pair

    def plan(pair, step):
        n_first = (pair + 1) // 2
        first = step < n_first
        return jnp.where(first, 0, 1), jnp.where(first, step, step - n_first)

    def moba_prepare(pair, par):
        for slot in range(2):
            qi = tile_of(pair, slot)
            qt = qta_ref[qi]
            past = blk < qi
            for hd in range(HEADS_PER_PAIR):
                qth = _head_rows(qt, hd)
                gate = (jnp.dot(km_hi, qth, preferred_element_type=_f32)
                        + jnp.dot(km_lo, qth, preferred_element_type=_f32))
                gate = jnp.where(past, gate, NEG_INF)
                chosen = jnp.zeros(gate.shape, _f32)
                for _ in range(MOBA_TOPK):
                    top = jnp.max(gate, axis=0, keepdims=True)
                    first = jnp.min(jnp.where(gate == top, blk_f, float(nq)), axis=0, keepdims=True)
                    pick = blk_f == first
                    chosen = jnp.where(pick, 1.0, chosen)
                    gate = jnp.where(pick, -jnp.inf, gate)
                sel = jnp.where(past, jnp.where(chosen > 0.0, 0.0, NEG_INF), NEG_INF)
                w_ref[slot, hd, :PAIR_DIM, :] = qth
                w_ref[slot, hd, PAIR_DIM:PAIR_DIM + nq, :] = sel.astype(_bf16)
                s = jnp.dot(ka_ref[qi], qth, preferred_element_type=_f32)
                s = jnp.where(causal, s, NEG_INF)
                sd_ref[par, slot, hd] = s
                ma_ref[par, slot, hd] = jnp.max(s, axis=0, keepdims=True)

    def moba_score_step(pair, par, step):
        slot, c = plan(pair, step)
        keys = ka_ref[pl.ds(2 * c, 2)].reshape(chunk, PAIR_DIM)
        lhs = jnp.concatenate([keys, aux_ref[c]], axis=1)
        for hd in range(HEADS_PER_PAIR):
            s = jnp.dot(lhs, w_ref[slot, hd], preferred_element_type=_f32)
            sa_ref[step, hd] = s
            ma_ref[par, slot, hd] = jnp.maximum(ma_ref[par, slot, hd], jnp.max(s, axis=0, keepdims=True))

    def moba_value_first(pair, par):
        for slot in range(2):
            qi = tile_of(pair, slot)
            for hd in range(HEADS_PER_PAIR):
                p = jnp.exp2(sd_ref[par, slot, hd] - ma_ref[par, slot, hd])
                acc_ref[slot, hd] += jnp.dot(moba_vt(qi, hd), p.astype(_bf16), preferred_element_type=_f32)

    def moba_value_step(pair, par, step):
        slot, c = plan(pair, step)
        for hd in range(HEADS_PER_PAIR):
            p = jnp.exp2(sa_ref[step, hd] - ma_ref[par, slot, hd])
            vtc = jnp.concatenate([moba_vt(2 * c, hd), moba_vt(2 * c + 1, hd)], axis=1)
            acc_ref[slot, hd] += jnp.dot(vtc, p.astype(_bf16), preferred_element_type=_f32)

    def moba_finish(pair):
        for slot in range(2):
            o_t = jnp.concatenate([acc_ref[slot, hd, :HEAD_DIM, :] / acc_ref[slot, hd, HEAD_DIM:HEAD_DIM + 1, :]
                                   for hd in range(HEADS_PER_PAIR)], axis=0)
            oa_ref[_tile_rows(tile_of(pair, slot)), :] = o_t.T.astype(oa_ref.dtype)

    for t in range(nq):
        rows = slice(t * ATT_TILE, (t + 1) * ATT_TILE)
        xq_ref[rows, :] = qtd_ref[t].astype(_f32).T
        xk_ref[rows, :] = kd_ref[t].astype(_f32)
        xv_ref[rows, :] = vtd_ref[t].astype(_f32).T

    def members(item, c):
        return pl.ds(CLASSES_PER_ITEM * item + c, ATT_TILE, stride=SPARSE_DILATION)

    def normalise(acc, m):
        l = acc[HEAD_DIM:HEAD_DIM + 1, :]
        return acc[:HEAD_DIM, :] / l, jnp.broadcast_to(m + jnp.log2(l), (HEAD_DIM, ATT_TILE))

    def class_score(item, par):
        band = band_ref[...]
        for c in range(CLASSES_PER_ITEM):
            qt = xq_ref[members(item, c), :].T.astype(_bf16)
            keys = xk_ref[members(item, c), :].astype(_bf16)
            vt3_ref[par, c] = xv_ref[members(item, c), :].T.astype(_bf16)
            for hd in range(HEADS_PER_PAIR):
                s = jnp.dot(keys, _head_rows(qt, hd), preferred_element_type=_f32) + band
                s3_ref[par, c, hd] = s
                m3_ref[par, c, hd] = jnp.max(s, axis=0, keepdims=True)

    def class_value(item, par):
        for c in range(CLASSES_PER_ITEM):
            outs, lses = [], []
            for hd in range(HEADS_PER_PAIR):
                m = m3_ref[par, c, hd]
                p = jnp.exp2(s3_ref[par, c, hd] - m)
                vth = jnp.concatenate([vt3_ref[par, c, hd * HEAD_DIM:(hd + 1) * HEAD_DIM, :], ones], axis=0)
                o, e = normalise(jnp.dot(vth, p.astype(_bf16), preferred_element_type=_f32), m)
                outs.append(o)
                lses.append(e)
            o3_ref[members(item, c), :] = jnp.concatenate(outs, axis=0).T
            e3_ref[members(item, c), :] = jnp.concatenate(lses, axis=0).T

    dense_ones = jnp.ones((ONES_ROWS, DENSE_TILES * ATT_TILE), _bf16)

    def first_tile(qi):
        return jnp.maximum(qi - (DENSE_TILES - 1), 0)

    def dense_score(item, t):
        qi = TILES_PER_ITEM * item + t
        j0 = first_tile(qi)
        keys = kd_ref[pl.ds(j0, DENSE_TILES)].reshape(DENSE_TILES * ATT_TILE, PAIR_DIM)
        bias = bias_ref[qi - j0]
        qt = qtd_ref[qi]
        for hd in range(HEADS_PER_PAIR):
            s = jnp.dot(keys, _head_rows(qt, hd), preferred_element_type=_f32) + bias
            s12_ref[t, hd] = s
            m12_ref[t, hd] = jnp.max(s, axis=0, keepdims=True)

    def dense_value(item, t):
        qi = TILES_PER_ITEM * item + t
        j0 = first_tile(qi)
        outs, lses = [], []
        for hd in range(HEADS_PER_PAIR):
            m = m12_ref[t, hd]
            p = jnp.exp2(s12_ref[t, hd] - m)
            vth = jnp.concatenate([vtd_ref[j0 + k, hd * HEAD_DIM:(hd + 1) * HEAD_DIM, :]
                                   for k in range(DENSE_TILES)], axis=1)
            o, e = normalise(jnp.dot(jnp.concatenate([vth, dense_ones], axis=0), p.astype(_bf16),
                                     preferred_element_type=_f32), m)
            outs.append(o)
            lses.append(e)
        o12_ref[_tile_rows(qi), :] = jnp.concatenate(outs, axis=0).T
        e12_ref[_tile_rows(qi), :] = jnp.concatenate(lses, axis=0).T

    def stage(score_item, value_item, score_par):
        value_par = 1 - score_par
        if value_item is not None:
            acc_ref[...] = jnp.zeros(acc_ref.shape, _f32)
        if score_item is not None:
            moba_prepare(score_item, score_par)
        for step in range(n_steps):
            if value_item is not None:
                moba_value_step(value_item, value_par, step)
            if score_item is not None:
                moba_score_step(score_item, score_par, step)
            if step in DENSE_AT_STEP:
                t = DENSE_AT_STEP.index(step)
                if value_item is not None:
                    dense_value(value_item, t)
                if score_item is not None:
                    dense_score(score_item, t)
            if step == CLASS_AT_STEP:
                if value_item is not None:
                    class_value(value_item, value_par)
                if score_item is not None:
                    class_score(score_item, score_par)
            if step == OWN_BLOCK_AT_STEP and value_item is not None:
                moba_value_first(value_item, value_par)
        if value_item is not None:
            moba_finish(value_item)

    _pair_pipeline(nq // 2, stage)

    for t in range(nq):
        rows = slice(t * ATT_TILE, (t + 1) * ATT_TILE)
        e12, e3 = e12_ref[rows, :], e3_ref[rows, :]
        top = jnp.maximum(e12, e3)
        w12 = jnp.exp2(e12 - top)
        w3 = jnp.exp2(e3 - top)
        od_ref[rows, :] = ((w12 * o12_ref[rows, :] + w3 * o3_ref[rows, :]) / (w12 + w3)).astype(od_ref.dtype)


def _mixers(qta, ka, vta, qtd, kd, vtd, aux, bias, band):
    b, n_pairs, nq = qta.shape[:3]
    s = nq * ATT_TILE
    assert nq >= DENSE_TILES and s // SPARSE_DILATION == ATT_TILE
    assert nq // 2 == SPARSE_DILATION // CLASSES_PER_ITEM == nq // TILES_PER_ITEM
    t_spec = pl.BlockSpec((None, None, nq, PAIR_DIM, ATT_TILE), lambda bi, pi: (bi, pi, 0, 0, 0))
    r_spec = pl.BlockSpec((None, None, nq, ATT_TILE, PAIR_DIM), lambda bi, pi: (bi, pi, 0, 0, 0))
    const = lambda a: pl.BlockSpec(a.shape, lambda bi, pi: (0,) * a.ndim, pipeline_mode=pl.Buffered(1))
    o_spec = pl.BlockSpec((None, s, PAIR_DIM), lambda bi, pi: (bi, 0, pi))
    o_shape = jax.ShapeDtypeStruct((b, s, n_pairs * PAIR_DIM), _bf16)
    token_major = lambda: pltpu.VMEM((s, PAIR_DIM), _f32)
    scratch = [
        pltpu.VMEM((nq, PAIR_DIM), _f32),
        pltpu.VMEM((2, HEADS_PER_PAIR, 2 * PAIR_DIM, ATT_TILE), _bf16),
        pltpu.VMEM((2, 2, HEADS_PER_PAIR, ATT_TILE, ATT_TILE), _f32),
        pltpu.VMEM((nq // 2, HEADS_PER_PAIR, 2 * ATT_TILE, ATT_TILE), _f32),
        pltpu.VMEM((2, 2, HEADS_PER_PAIR, 1, ATT_TILE), _f32),
        pltpu.VMEM((2, HEADS_PER_PAIR, HEAD_DIM + ONES_ROWS, ATT_TILE), _f32),
        pltpu.VMEM((TILES_PER_ITEM, HEADS_PER_PAIR, DENSE_TILES * ATT_TILE, ATT_TILE), _f32),
        pltpu.VMEM((TILES_PER_ITEM, HEADS_PER_PAIR, 1, ATT_TILE), _f32),
        token_major(), token_major(), token_major(),
        token_major(), token_major(),
        pltpu.VMEM((2, CLASSES_PER_ITEM, HEADS_PER_PAIR, ATT_TILE, ATT_TILE), _f32),
        pltpu.VMEM((2, CLASSES_PER_ITEM, HEADS_PER_PAIR, 1, ATT_TILE), _f32),
        pltpu.VMEM((2, CLASSES_PER_ITEM, PAIR_DIM, ATT_TILE), _bf16),
        token_major(), token_major(),
    ]
    return pl.pallas_call(
        _mixers_kernel,
        grid=(b, n_pairs),
        in_specs=[t_spec, r_spec, t_spec, const(aux), t_spec, r_spec, t_spec, const(bias), const(band)],
        out_specs=[o_spec, o_spec],
        out_shape=[o_shape, o_shape],
        scratch_shapes=scratch,
        compiler_params=pltpu.CompilerParams(
            dimension_semantics=("parallel", "parallel"), vmem_limit_bytes=VMEM_LIMIT_BYTES),
        name="mixers",
    )(qta, ka, vta, aux, qtd, kd, vtd, bias, band)


def _post_kernel(x_ref, oa_ref, od_ref, ga_ref, gd_ref, wout_ref, gm_ref, wup_ref, wdown_ref,
                 gf_ref, o_ref, *, final_norm):
    da = oa_ref.shape[1]
    na = _rms_norm_rows(oa_ref[...].astype(_f32), ga_ref[...]).astype(_bf16)
    nd = _rms_norm_rows(od_ref[...].astype(_f32), gd_ref[...]).astype(_bf16)
    y = (x_ref[...]
         + jnp.dot(na, wout_ref[:da, :], preferred_element_type=_f32)
         + jnp.dot(nd, wout_ref[da:, :], preferred_element_type=_f32))
    h = _rms_norm_rows(y, gm_ref[...]).astype(_bf16)
    u = jnp.dot(h, wup_ref[...], preferred_element_type=_f32)
    u = jnp.square(jnp.maximum(u, 0.0)).astype(_bf16)
    y = y + jnp.dot(u, wdown_ref[...], preferred_element_type=_f32)
    if final_norm:
        y = _rms_norm_rows(y, gf_ref[...])
    o_ref[...] = y


def _post(x, oa, od, ga, gd, wout_all, gm, wup_all, wdown_all, gf, layer, *, tm, final_norm):
    b, s, d = x.shape
    da = oa.shape[-1]
    dd = od.shape[-1]
    row = lambda bi, si: (bi, si, 0)
    const = lambda bi, si: (0, 0)
    whole = lambda a: pl.BlockSpec((None,) + a.shape[1:], lambda bi, si: (layer, 0, 0),
                                   pipeline_mode=pl.Buffered(1))
    return pl.pallas_call(
        functools.partial(_post_kernel, final_norm=final_norm),
        grid=(b, s // tm),
        in_specs=[
            pl.BlockSpec((None, tm, d), row),
            pl.BlockSpec((None, tm, da), row),
            pl.BlockSpec((None, tm, dd), row),
            pl.BlockSpec((1, da), const),
            pl.BlockSpec((1, dd), const),
            whole(wout_all),
            pl.BlockSpec((1, d), const),
            whole(wup_all),
            whole(wdown_all),
            pl.BlockSpec((1, d), const),
        ],
        out_specs=pl.BlockSpec((None, tm, d), row),
        out_shape=jax.ShapeDtypeStruct((b, s, d), _f32),
        compiler_params=pltpu.CompilerParams(
            dimension_semantics=("parallel", "parallel"), vmem_limit_bytes=VMEM_LIMIT_BYTES),
        name="post",
    )(x, oa, od, ga, gd, wout_all, gm, wup_all, wdown_all, gf)


def _rope_tables(seq):
    inv = 1.0 / (ROPE_THETA ** (jnp.arange(0, HEAD_DIM, 2, dtype=_f32) / HEAD_DIM))
    ang = jnp.arange(seq, dtype=_f32)[:, None] * inv[None, :]
    cos, sin = jnp.cos(ang), jnp.sin(ang)
    reps = PAIR_DIM // (HEAD_DIM // 2)
    cos_t = jnp.tile(cos, (1, reps))
    sin_t = jnp.tile(jnp.concatenate([-sin, sin], axis=-1), (1, HEADS_PER_PAIR))
    return cos_t, sin_t


def kernel(x, attn_norm, w_in, moba_out_norm, dil_out_norm, w_out, mlp_norm, w_up, w_down, final_norm):
    depth = w_in.shape[0]
    seq = x.shape[1]
    cos_t, sin_t = _rope_tables(seq)
    aux = _moba_block_onehot(seq // ATT_TILE)
    bias = _dense_bias_table()
    band = _class_band_table()
    row = lambda v: v.reshape(1, -1)
    w_in, w_out, w_up, w_down = (w.astype(_bf16) for w in (w_in, w_out, w_up, w_down))
    for l in range(depth):
        qta, ka, vta, qtd, kd, vtd = _inproj(x, row(attn_norm[l]), w_in, l, cos_t, sin_t, tm=1024)
        out_a, out_d = _mixers(qta, ka, vta, qtd, kd, vtd, aux, bias, band)
        x = _post(x, out_a, out_d, row(moba_out_norm[l]), row(dil_out_norm[l]), w_out, row(mlp_norm[l]),
                  w_up, w_down, row(final_norm), l, tm=512, final_norm=(l == depth - 1))
    return x
```

```python
import functools
import math

import jax
import jax.numpy as jnp
from jax import lax
from jax.experimental import pallas as pl
from jax.experimental.pallas import tpu as pltpu

HEAD_DIM = 64
HEADS_PER_PAIR = 2
PAIR_DIM = HEADS_PER_PAIR * HEAD_DIM
MOBA_BLOCK = 256
MOBA_TOPK = 3
DILATED_CONFIGS = ((128, 1), (512, 4), (2048, 16))
ROPE_THETA = 10000.0
NORM_EPS = 1e-6
NEG_INF = -1e30
ATT_TILE = 256
ONES_ROWS = 16
VMEM_LIMIT_BYTES = 60 * 1024 * 1024

_f32 = jnp.float32
_bf16 = jnp.bfloat16


def _rms_norm_rows(x, g):
    return x * lax.rsqrt(jnp.mean(x * x, axis=-1, keepdims=True) + NORM_EPS) * g


def _inproj_kernel(x_ref, g_ref, w_ref, cos_ref, sin_ref,
                   qta_ref, ka_ref, vta_ref, qtd_ref, kd_ref, vtd_ref, qrd_ref, vrd_ref, *, n_pairs, scale):
    tm = x_ref.shape[0]
    nt = tm // ATT_TILE
    h = _rms_norm_rows(x_ref[...], g_ref[...]).astype(_bf16)
    group = n_pairs * PAIR_DIM
    cos = cos_ref[...]
    sin = sin_ref[...]
    lane = lax.broadcasted_iota(jnp.int32, (tm, PAIR_DIM), 1)
    first_half = (lane % HEAD_DIM) < (HEAD_DIM // 2)

    def rope(c):
        partner = jnp.where(first_half,
                            pltpu.roll(c, PAIR_DIM - HEAD_DIM // 2, 1),
                            pltpu.roll(c, HEAD_DIM // 2, 1))
        return c * cos + partner * sin

    def project(g):
        return jnp.dot(h, w_ref[:, g * group:(g + 1) * group], preferred_element_type=_f32)

    def put_transposed(ref, p, c):
        for t in range(nt):
            ref[p, t] = c[t * ATT_TILE:(t + 1) * ATT_TILE, :].T.astype(_bf16)

    def put_rows(ref, p, c):
        for t in range(nt):
            ref[p, t] = c[t * ATT_TILE:(t + 1) * ATT_TILE, :].astype(_bf16)

    def put_both(t_ref, r_ref, p, c):
        put_transposed(t_ref, p, c)
        put_rows(r_ref, p, c)

    plan = ((functools.partial(put_transposed, qta_ref), lambda c: rope(c) * scale),
            (functools.partial(put_rows, ka_ref), rope),
            (functools.partial(put_transposed, vta_ref), lambda c: c),
            (functools.partial(put_both, qtd_ref, qrd_ref), lambda c: rope(c) * scale),
            (functools.partial(put_rows, kd_ref), rope),
            (functools.partial(put_both, vtd_ref, vrd_ref), lambda c: c))
    for g, (put, fn) in enumerate(plan):
        proj = project(g)
        for p in range(n_pairs):
            put(p, fn(proj[:, p * PAIR_DIM:(p + 1) * PAIR_DIM]))


def _inproj(x, g, w_all, layer, cos_t, sin_t, *, tm):
    b, s, d = x.shape
    n_pairs = w_all.shape[2] // (6 * PAIR_DIM)
    nq = s // ATT_TILE
    nt = tm // ATT_TILE
    t_shape = jax.ShapeDtypeStruct((b, n_pairs, nq, PAIR_DIM, ATT_TILE), _bf16)
    r_shape = jax.ShapeDtypeStruct((b, n_pairs, nq, ATT_TILE, PAIR_DIM), _bf16)
    t_spec = pl.BlockSpec((None, n_pairs, nt, PAIR_DIM, ATT_TILE), lambda bi, si: (bi, 0, si, 0, 0))
    r_spec = pl.BlockSpec((None, n_pairs, nt, ATT_TILE, PAIR_DIM), lambda bi, si: (bi, 0, si, 0, 0))
    return pl.pallas_call(
        functools.partial(_inproj_kernel, n_pairs=n_pairs, scale=HEAD_DIM ** -0.5 * math.log2(math.e)),
        grid=(b, s // tm),
        in_specs=[
            pl.BlockSpec((None, tm, d), lambda bi, si: (bi, si, 0)),
            pl.BlockSpec((1, d), lambda bi, si: (0, 0)),
            pl.BlockSpec((None,) + w_all.shape[1:], lambda bi, si: (layer, 0, 0), pipeline_mode=pl.Buffered(1)),
            pl.BlockSpec((tm, PAIR_DIM), lambda bi, si: (si, 0)),
            pl.BlockSpec((tm, PAIR_DIM), lambda bi, si: (si, 0)),
        ],
        out_specs=[t_spec, r_spec, t_spec, t_spec, r_spec, t_spec, r_spec, r_spec],
        out_shape=[t_shape, r_shape, t_shape, t_shape, r_shape, t_shape, r_shape, r_shape],
        compiler_params=pltpu.CompilerParams(
            dimension_semantics=("parallel", "parallel"), vmem_limit_bytes=VMEM_LIMIT_BYTES),
        name="inproj",
    )(x, g, w_all, cos_t, sin_t)


def _head_rows(qt, head):
    row = lax.broadcasted_iota(jnp.int32, qt.shape, 0)
    mine = (row >= head * HEAD_DIM) & (row < (head + 1) * HEAD_DIM)
    return jnp.where(mine, qt, jnp.zeros_like(qt))


def _pair_pipeline(n_pairs, stage):
    assert n_pairs >= 2 and n_pairs % 2 == 0
    stage(0, None, 0)

    def two_pairs(u, carry):
        stage(2 * u + 1, 2 * u, 1)
        stage(2 * u + 2, 2 * u + 1, 0)
        return carry

    lax.fori_loop(0, n_pairs // 2 - 1, two_pairs, 0)
    stage(n_pairs - 1, n_pairs - 2, 1)
    stage(None, n_pairs - 1, 0)


DENSE_CONFIGS = DILATED_CONFIGS[:2]
SPARSE_WINDOW, SPARSE_DILATION = DILATED_CONFIGS[2]
DENSE_TILES = max(w for w, _ in DENSE_CONFIGS) // ATT_TILE + 1
CLASSES_PER_ITEM = 2
TILES_PER_ITEM = 2
DENSE_AT_STEP = (1, 5)
CLASS_AT_STEP = 3
OWN_BLOCK_AT_STEP = 6


def _moba_block_onehot(nq):
    c = jnp.arange(nq // 2, dtype=jnp.int32)[:, None, None]
    r = jnp.arange(2 * ATT_TILE, dtype=jnp.int32)[None, :, None]
    lane = jnp.arange(PAIR_DIM, dtype=jnp.int32)[None, None, :]
    return (lane == 2 * c + r // ATT_TILE).astype(_bf16)


def _dense_bias_table():
    v = jnp.arange(DENSE_TILES, dtype=jnp.int32)[:, None, None]
    row = jnp.arange(DENSE_TILES * ATT_TILE, dtype=jnp.int32)[None, :, None]
    qry = jnp.arange(ATT_TILE, dtype=jnp.int32)[None, None, :]
    delta = v * ATT_TILE + qry - row
    count = jnp.zeros(delta.shape, _f32)
    for window, dilation in DENSE_CONFIGS:
        hit = (delta >= 0) & (delta <= window) & (delta % dilation == 0)
        count = count + hit.astype(_f32)
    return jnp.where(count > 0, jnp.log2(jnp.maximum(count, 1.0)), NEG_INF)


def _class_band_table():
    key = jnp.arange(ATT_TILE, dtype=jnp.int32)[:, None]
    qry = jnp.arange(ATT_TILE, dtype=jnp.int32)[None, :]
    delta = qry - key
    return jnp.where((delta >= 0) & (delta <= SPARSE_WINDOW // SPARSE_DILATION), 0.0, NEG_INF).astype(_f32)


def _tile_rows(qi):
    row0 = qi * ATT_TILE
    if not isinstance(row0, int):
        row0 = pl.multiple_of(row0, ATT_TILE)
    return pl.ds(row0, ATT_TILE)


def _mixers_kernel(qta_ref, ka_ref, vta_ref, aux_ref, qtd_ref, kd_ref, vtd_ref, qrd_ref, vrd_ref, bias_ref, band_ref,
                   oa_ref, od_ref,
                   kmean_ref, w_ref, sd_ref, sa_ref, ma_ref, acc_ref,
                   s12_ref, m12_ref, xq_ref, xk_ref, xv_ref, o3_ref, e3_ref, s3_ref, m3_ref, vt3_ref,
                   o12_ref, e12_ref):
    nq = qta_ref.shape[0]
    n_steps = nq // 2
    chunk = 2 * ATT_TILE
    ones = jnp.ones((ONES_ROWS, ATT_TILE), _bf16)

    for j in range(nq):
        kmean_ref[j:j + 1, :] = jnp.mean(ka_ref[j].astype(_f32), axis=0, keepdims=True)
    km = kmean_ref[...]
    km_hi = km.astype(_bf16)
    km_lo = (km - km_hi.astype(_f32)).astype(_bf16)
    for slot in range(2):
        for hd in range(HEADS_PER_PAIR):
            w_ref[slot, hd, PAIR_DIM + nq:, :] = jnp.zeros((PAIR_DIM - nq, ATT_TILE), _bf16)

    key_i = lax.broadcasted_iota(jnp.int32, (ATT_TILE, ATT_TILE), 0)
    qry_i = lax.broadcasted_iota(jnp.int32, (ATT_TILE, ATT_TILE), 1)
    causal = key_i <= qry_i
    blk = lax.broadcasted_iota(jnp.int32, (nq, ATT_TILE), 0)
    blk_f = blk.astype(_f32)

    def moba_vt(j, hd):
        return jnp.concatenate([vta_ref[j, hd * HEAD_DIM:(hd + 1) * HEAD_DIM, :], ones], axis=0)

    def tile_of(pair, slot):
        return pair if slot == 0 else nq - 1 - pair

    def plan(pair, step):
        n_first = (pair + 1) // 2
        first = step < n_first
        return jnp.where(first, 0, 1), jnp.where(first, step, step - n_first)

    def moba_prepare(pair, par):
        for slot in range(2):
            qi = tile_of(pair, slot)
            qt = qta_ref[qi]
            past = blk < qi
            for hd in range(HEADS_PER_PAIR):
                qth = _head_rows(qt, hd)
                gate = (jnp.dot(km_hi, qth, preferred_element_type=_f32)
                        + jnp.dot(km_lo, qth, preferred_element_type=_f32))
                gate = jnp.where(past, gate, NEG_INF)
                chosen = jnp.zeros(gate.shape, _f32)
                for _ in range(MOBA_TOPK):
                    top = jnp.max(gate, axis=0, keepdims=True)
                    first = jnp.min(jnp.where(gate == top, blk_f, float(nq)), axis=0, keepdims=True)
                    pick = blk_f == first
                    chosen = jnp.where(pick, 1.0, chosen)
                    gate = jnp.where(pick, -jnp.inf, gate)
                sel = jnp.where(past, jnp.where(chosen > 0.0, 0.0, NEG_INF), NEG_INF)
                w_ref[slot, hd, :PAIR_DIM, :] = qth
                w_ref[slot, hd, PAIR_DIM:PAIR_DIM + nq, :] = sel.astype(_bf16)
                s = jnp.dot(ka_ref[qi], qth, preferred_element_type=_f32)
                s = jnp.where(causal, s, NEG_INF)
                sd_ref[par, slot, hd] = s
                ma_ref[par, slot, hd] = jnp.max(s, axis=0, keepdims=True)

    def moba_score_step(pair, par, step):
        slot, c = plan(pair, step)
        keys = ka_ref[pl.ds(2 * c, 2)].reshape(chunk, PAIR_DIM)
        lhs = jnp.concatenate([keys, aux_ref[c]], axis=1)
        for hd in range(HEADS_PER_PAIR):
            s = jnp.dot(lhs, w_ref[slot, hd], preferred_element_type=_f32)
            sa_ref[step, hd] = s
            ma_ref[par, slot, hd] = jnp.maximum(ma_ref[par, slot, hd], jnp.max(s, axis=0, keepdims=True))

    def moba_value_first(pair, par):
        for slot in range(2):
            qi = tile_of(pair, slot)
            for hd in range(HEADS_PER_PAIR):
                p = jnp.exp2(sd_ref[par, slot, hd] - ma_ref[par, slot, hd])
                acc_ref[slot, hd] += jnp.dot(moba_vt(qi, hd), p.astype(_bf16), preferred_element_type=_f32)

    def moba_value_step(pair, par, step):
        slot, c = plan(pair, step)
        for hd in range(HEADS_PER_PAIR):
            p = jnp.exp2(sa_ref[step, hd] - ma_ref[par, slot, hd])
            vtc = jnp.concatenate([moba_vt(2 * c, hd), moba_vt(2 * c + 1, hd)], axis=1)
            acc_ref[slot, hd] += jnp.dot(vtc, p.astype(_bf16), preferred_element_type=_f32)

    def moba_finish(pair):
        for slot in range(2):
            o_t = jnp.concatenate([acc_ref[slot, hd, :HEAD_DIM, :] / acc_ref[slot, hd, HEAD_DIM:HEAD_DIM + 1, :]
                                   for hd in range(HEADS_PER_PAIR)], axis=0)
            oa_ref[_tile_rows(tile_of(pair, slot)), :] = o_t.T.astype(oa_ref.dtype)

    for t in range(nq):
        rows = slice(t * ATT_TILE, (t + 1) * ATT_TILE)
        xq_ref[rows, :] = qrd_ref[t].astype(_f32)
        xk_ref[rows, :] = kd_ref[t].astype(_f32)
        xv_ref[rows, :] = vrd_ref[t].astype(_f32)

    def members(item, c):
        return pl.ds(CLASSES_PER_ITEM * item + c, ATT_TILE, stride=SPARSE_DILATION)

    def normalise(acc, m):
        l = acc[HEAD_DIM:HEAD_DIM + 1, :]
        return acc[:HEAD_DIM, :] / l, jnp.broadcast_to(m + jnp.log2(l), (HEAD_DIM, ATT_TILE))

    def class_score(item, par):
        band = band_ref[...]
        for c in range(CLASSES_PER_ITEM):
            qt = xq_ref[members(item, c), :].T.astype(_bf16)
            keys = xk_ref[members(item, c), :].astype(_bf16)
            vt3_ref[par, c] = xv_ref[members(item, c), :].T.astype(_bf16)
            for hd in range(HEADS_PER_PAIR):
                s = jnp.dot(keys, _head_rows(qt, hd), preferred_element_type=_f32) + band
                s3_ref[par, c, hd] = s
                m3_ref[par, c, hd] = jnp.max(s, axis=0, keepdims=True)

    def class_value(item, par):
        for c in range(CLASSES_PER_ITEM):
            outs, lses = [], []
            for hd in range(HEADS_PER_PAIR):
                m = m3_ref[par, c, hd]
                p = jnp.exp2(s3_ref[par, c, hd] - m)
                vth = jnp.concatenate([vt3_ref[par, c, hd * HEAD_DIM:(hd + 1) * HEAD_DIM, :], ones], axis=0)
                o, e = normalise(jnp.dot(vth, p.astype(_bf16), preferred_element_type=_f32), m)
                outs.append(o)
                lses.append(e)
            o3_ref[members(item, c), :] = jnp.concatenate(outs, axis=0).T
            e3_ref[members(item, c), :] = jnp.concatenate(lses, axis=0).T

    dense_ones = jnp.ones((ONES_ROWS, DENSE_TILES * ATT_TILE), _bf16)

    def first_tile(qi):
        return jnp.maximum(qi - (DENSE_TILES - 1), 0)

    def dense_score(item, t):
        qi = TILES_PER_ITEM * item + t
        j0 = first_tile(qi)
        keys = kd_ref[pl.ds(j0, DENSE_TILES)].reshape(DENSE_TILES * ATT_TILE, PAIR_DIM)
        bias = bias_ref[qi - j0]
        qt = qtd_ref[qi]
        for hd in range(HEADS_PER_PAIR):
            s = jnp.dot(keys, _head_rows(qt, hd), preferred_element_type=_f32) + bias
            s12_ref[t, hd] = s
            m12_ref[t, hd] = jnp.max(s, axis=0, keepdims=True)

    def dense_value(item, t):
        qi = TILES_PER_ITEM * item + t
        j0 = first_tile(qi)
        outs, lses = [], []
        for hd in range(HEADS_PER_PAIR):
            m = m12_ref[t, hd]
            p = jnp.exp2(s12_ref[t, hd] - m)
            vth = jnp.concatenate([vtd_ref[j0 + k, hd * HEAD_DIM:(hd + 1) * HEAD_DIM, :]
                                   for k in range(DENSE_TILES)], axis=1)
            o, e = normalise(jnp.dot(jnp.concatenate([vth, dense_ones], axis=0), p.astype(_bf16),
                                     preferred_element_type=_f32), m)
            outs.append(o)
            lses.append(e)
        o12_ref[_tile_rows(qi), :] = jnp.concatenate(outs, axis=0).T
        e12_ref[_tile_rows(qi), :] = jnp.concatenate(lses, axis=0).T

    def stage(score_item, value_item, score_par):
        value_par = 1 - score_par
        if value_item is not None:
            acc_ref[...] = jnp.zeros(acc_ref.shape, _f32)
        if score_item is not None:
            moba_prepare(score_item, score_par)
        for step in range(n_steps):
            if value_item is not None:
                moba_value_step(value_item, value_par, step)
            if score_item is not None:
                moba_score_step(score_item, score_par, step)
            if step in DENSE_AT_STEP:
                t = DENSE_AT_STEP.index(step)
                if value_item is not None:
                    dense_value(value_item, t)
                if score_item is not None:
                    dense_score(score_item, t)
            if step == CLASS_AT_STEP:
                if value_item is not None:
                    class_value(value_item, value_par)
                if score_item is not None:
                    class_score(score_item, score_par)
            if step == OWN_BLOCK_AT_STEP and value_item is not None:
                moba_value_first(value_item, value_par)
        if value_item is not None:
            moba_finish(value_item)

    _pair_pipeline(nq // 2, stage)

    for t in range(nq):
        rows = slice(t * ATT_TILE, (t + 1) * ATT_TILE)
        e12, e3 = e12_ref[rows, :], e3_ref[rows, :]
        top = jnp.maximum(e12, e3)
        w12 = jnp.exp2(e12 - top)
        w3 = jnp.exp2(e3 - top)
        od_ref[rows, :] = ((w12 * o12_ref[rows, :] + w3 * o3_ref[rows, :]) / (w12 + w3)).astype(od_ref.dtype)


def _mixers(qta, ka, vta, qtd, kd, vtd, qrd, vrd, aux, bias, band):
    b, n_pairs, nq = qta.shape[:3]
    s = nq * ATT_TILE
    assert nq >= DENSE_TILES and s // SPARSE_DILATION == ATT_TILE
    assert nq // 2 == SPARSE_DILATION // CLASSES_PER_ITEM == nq // TILES_PER_ITEM
    t_spec = pl.BlockSpec((None, None, nq, PAIR_DIM, ATT_TILE), lambda bi, pi: (bi, pi, 0, 0, 0))
    r_spec = pl.BlockSpec((None, None, nq, ATT_TILE, PAIR_DIM), lambda bi, pi: (bi, pi, 0, 0, 0))
    const = lambda a: pl.BlockSpec(a.shape, lambda bi, pi: (0,) * a.ndim, pipeline_mode=pl.Buffered(1))
    o_spec = pl.BlockSpec((None, s, PAIR_DIM), lambda bi, pi: (bi, 0, pi))
    o_shape = jax.ShapeDtypeStruct((b, s, n_pairs * PAIR_DIM), _bf16)
    token_major = lambda: pltpu.VMEM((s, PAIR_DIM), _f32)
    scratch = [
        pltpu.VMEM((nq, PAIR_DIM), _f32),
        pltpu.VMEM((2, HEADS_PER_PAIR, 2 * PAIR_DIM, ATT_TILE), _bf16),
        pltpu.VMEM((2, 2, HEADS_PER_PAIR, ATT_TILE, ATT_TILE), _f32),
        pltpu.VMEM((nq // 2, HEADS_PER_PAIR, 2 * ATT_TILE, ATT_TILE), _f32),
        pltpu.VMEM((2, 2, HEADS_PER_PAIR, 1, ATT_TILE), _f32),
        pltpu.VMEM((2, HEADS_PER_PAIR, HEAD_DIM + ONES_ROWS, ATT_TILE), _f32),
        pltpu.VMEM((TILES_PER_ITEM, HEADS_PER_PAIR, DENSE_TILES * ATT_TILE, ATT_TILE), _f32),
        pltpu.VMEM((TILES_PER_ITEM, HEADS_PER_PAIR, 1, ATT_TILE), _f32),
        token_major(), token_major(), token_major(),
        token_major(), token_major(),
        pltpu.VMEM((2, CLASSES_PER_ITEM, HEADS_PER_PAIR, ATT_TILE, ATT_TILE), _f32),
        pltpu.VMEM((2, CLASSES_PER_ITEM, HEADS_PER_PAIR, 1, ATT_TILE), _f32),
        pltpu.VMEM((2, CLASSES_PER_ITEM, PAIR_DIM, ATT_TILE), _bf16),
        token_major(), token_major(),
    ]
    return pl.pallas_call(
        _mixers_kernel,
        grid=(b, n_pairs),
        in_specs=[t_spec, r_spec, t_spec, const(aux), t_spec, r_spec, t_spec, r_spec, r_spec, const(bias), const(band)],
        out_specs=[o_spec, o_spec],
        out_shape=[o_shape, o_shape],
        scratch_shapes=scratch,
        compiler_params=pltpu.CompilerParams(
            dimension_semantics=("parallel", "parallel"), vmem_limit_bytes=VMEM_LIMIT_BYTES),
        name="mixers",
    )(qta, ka, vta, aux, qtd, kd, vtd, qrd, vrd, bias, band)


def _post_kernel(x_ref, oa_ref, od_ref, ga_ref, gd_ref, wout_ref, gm_ref, wup_ref, wdown_ref,
                 gf_ref, o_ref, *, final_norm):
    da = oa_ref.shape[1]
    na = _rms_norm_rows(oa_ref[...].astype(_f32), ga_ref[...]).astype(_bf16)
    nd = _rms_norm_rows(od_ref[...].astype(_f32), gd_ref[...]).astype(_bf16)
    y = (x_ref[...]
         + jnp.dot(na, wout_ref[:da, :], preferred_element_type=_f32)
         + jnp.dot(nd, wout_ref[da:, :], preferred_element_type=_f32))
    h = _rms_norm_rows(y, gm_ref[...]).astype(_bf16)
    u = jnp.dot(h, wup_ref[...], preferred_element_type=_f32)
    u = jnp.square(jnp.maximum(u, 0.0)).astype(_bf16)
    y = y + jnp.dot(u, wdown_ref[...], preferred_element_type=_f32)
    if final_norm:
        y = _rms_norm_rows(y, gf_ref[...])
    o_ref[...] = y


def _post(x, oa, od, ga, gd, wout_all, gm, wup_all, wdown_all, gf, layer, *, tm, final_norm):
    b, s, d = x.shape
    da = oa.shape[-1]
    dd = od.shape[-1]
    row = lambda bi, si: (bi, si, 0)
    const = lambda bi, si: (0, 0)
    whole = lambda a: pl.BlockSpec((None,) + a.shape[1:], lambda bi, si: (layer, 0, 0),
                                   pipeline_mode=pl.Buffered(1))
    return pl.pallas_call(
        functools.partial(_post_kernel, final_norm=final_norm),
        grid=(b, s // tm),
        in_specs=[
            pl.BlockSpec((None, tm, d), row),
            pl.BlockSpec((None, tm, da), row),
            pl.BlockSpec((None, tm, dd), row),
            pl.BlockSpec((1, da), const),
            pl.BlockSpec((1, dd), const),
            whole(wout_all),
            pl.BlockSpec((1, d), const),
            whole(wup_all),
            whole(wdown_all),
            pl.BlockSpec((1, d), const),
        ],
        out_specs=pl.BlockSpec((None, tm, d), row),
        out_shape=jax.ShapeDtypeStruct((b, s, d), _f32),
        compiler_params=pltpu.CompilerParams(
            dimension_semantics=("parallel", "parallel"), vmem_limit_bytes=VMEM_LIMIT_BYTES),
        name="post",
    )(x, oa, od, ga, gd, wout_all, gm, wup_all, wdown_all, gf)


def _rope_tables(seq):
    inv = 1.0 / (ROPE_THETA ** (jnp.arange(0, HEAD_DIM, 2, dtype=_f32) / HEAD_DIM))
    ang = jnp.arange(seq, dtype=_f32)[:, None] * inv[None, :]
    cos, sin = jnp.cos(ang), jnp.sin(ang)
    reps = PAIR_DIM // (HEAD_DIM // 2)
    cos_t = jnp.tile(cos, (1, reps))
    sin_t = jnp.tile(jnp.concatenate([-sin, sin], axis=-1), (1, HEADS_PER_PAIR))
    return cos_t, sin_t


def kernel(x, attn_norm, w_in, moba_out_norm, dil_out_norm, w_out, mlp_norm, w_up, w_down, final_norm):
    depth = w_in.shape[0]
    seq = x.shape[1]
    cos_t, sin_t = _rope_tables(seq)
    aux = _moba_block_onehot(seq // ATT_TILE)
    bias = _dense_bias_table()
    band = _class_band_table()
    row = lambda v: v.reshape(1, -1)
    w_in, w_out, w_up, w_down = (w.astype(_bf16) for w in (w_in, w_out, w_up, w_down))
    for l in range(depth):
        qta, ka, vta, qtd, kd, vtd, qrd, vrd = _inproj(x, row(attn_norm[l]), w_in, l, cos_t, sin_t, tm=1024)
        out_a, out_d = _mixers(qta, ka, vta, qtd, kd, vtd, qrd, vrd, aux, bias, band)
        x = _post(x, out_a, out_d, row(moba_out_norm[l]), row(dil_out_norm[l]), w_out, row(mlp_norm[l]),
                  w_up, w_down, row(final_norm), l, tm=512, final_norm=(l == depth - 1))
    return x
```
